```python
import math
import jax, jax.numpy as jnp
from jax import lax
import numpy as np

D_MODEL = 1024
BATCH = 2
SEQ = 8192
DEPTH = 4

D_RNN = 1024
N_RNN_BLOCKS = 8
RNN_BLOCK = D_RNN // N_RNN_BLOCKS
CONV_WIDTH = 4
LRU_C = 8.0
N_HEADS = 8
N_KV_HEADS = 2
GROUP = N_HEADS // N_KV_HEADS
HEAD_DIM = 128
WINDOW = 128
BLOCK = 128
D_ATTN = N_HEADS * HEAD_DIM
D_KV = N_KV_HEADS * HEAD_DIM
N_BRANCH = 2
D_BRANCH = 1024
OFF_XR = 0
OFF_GR = OFF_XR + D_RNN
OFF_Q = OFF_GR + D_RNN
OFF_K = OFF_Q + D_ATTN
OFF_V = OFF_K + D_KV
OFF_G = OFF_V + D_KV
D_IN = OFF_G + N_BRANCH * D_MODEL
D_FF = 3 * D_MODEL
N_EXPERTS = 8
TOP_K = 2
D_FF_EXPERT = D_FF // 2
N_DENSE = (DEPTH + 1) // 2
N_MOE = DEPTH // 2
EPS = 1e-6

kernel_name = "hybrid_rglru_swa_moe_trunk"


def rms_norm(x, gain):
    xf = x.astype(jnp.float32)
    y = xf * lax.rsqrt(jnp.mean(xf * xf, axis=-1, keepdims=True) + EPS)
    return (y * gain.astype(jnp.float32)).astype(x.dtype)


def causal_depthwise_conv(x, w, b):
    out = lax.conv_general_dilated(
        x, w[:, None, :].astype(x.dtype), window_strides=(1,),
        padding=[(CONV_WIDTH - 1, 0)], dimension_numbers=('NWC', 'WIO', 'NWC'),
        feature_group_count=x.shape[-1])
    return out + b


def rg_lru(x, w_a, b_a, w_x, b_x, lam):
    B, S, _ = x.shape
    xb = x.reshape(B, S, N_RNN_BLOCKS, RNN_BLOCK)
    r = jax.nn.sigmoid((jnp.einsum('bsni,nij->bsnj', xb, w_a).reshape(B, S, D_RNN) + b_a).astype(jnp.float32))
    i = jax.nn.sigmoid((jnp.einsum('bsni,nij->bsnj', xb, w_x).reshape(B, S, D_RNN) + b_x).astype(jnp.float32))
    log_a = -LRU_C * r * jax.nn.softplus(-lam.astype(jnp.float32))
    a = jnp.exp(log_a)
    mult = jnp.sqrt(-jnp.expm1(2.0 * log_a))
    u = mult * i * x.astype(jnp.float32)

    def combine(left, right):
        a1, b1 = left
        a2, b2 = right
        return a1 * a2, a2 * b1 + b2

    _, h = lax.associative_scan(combine, (a, u), axis=1)
    return h.astype(x.dtype)


def alibi_slopes():
    return jnp.asarray([2.0 ** (-8.0 * (h + 1) / N_HEADS) for h in range(N_HEADS)], dtype=jnp.float32)


def sliding_window_attention(q, k, v, sinks):
    B, S = q.shape[0], q.shape[1]
    nb = S // BLOCK
    qb = q.reshape(B, nb, BLOCK, N_KV_HEADS, GROUP, HEAD_DIM)
    kb = k.reshape(B, nb, BLOCK, N_KV_HEADS, HEAD_DIM)
    vb = v.reshape(B, nb, BLOCK, N_KV_HEADS, HEAD_DIM)

    def with_prev(t):
        prev = jnp.pad(t[:, :-1], ((0, 0), (1, 0), (0, 0), (0, 0), (0, 0)))
        return jnp.concatenate([prev, t], axis=2)

    kk, vv = with_prev(kb), with_prev(vb)
    scores = jnp.einsum('bnqkgd,bnskd->bnkgqs', qb, kk).astype(jnp.float32) * (HEAD_DIM ** -0.5)
    qi = jnp.arange(BLOCK)[:, None]
    sj = jnp.arange(2 * BLOCK)[None, :]
    dist = (qi + BLOCK - sj).astype(jnp.float32)
    key_pos = jnp.arange(nb)[:, None] * BLOCK - BLOCK + jnp.arange(2 * BLOCK)[None, :]
    valid = ((dist >= 0) & (dist < WINDOW))[None, :, :] & (key_pos >= 0)[:, None, :]
    slopes = alibi_slopes().reshape(N_KV_HEADS, GROUP)
    scores = scores - slopes[:, :, None, None] * dist
    scores = jnp.where(valid[None, :, None, None], scores, jnp.finfo(jnp.float32).min)
    sink = jnp.broadcast_to(sinks.astype(jnp.float32).reshape(N_KV_HEADS, GROUP)[None, None, :, :, None, None],
                            scores.shape[:-1] + (1,))
    probs = jax.nn.softmax(jnp.concatenate([scores, sink], axis=-1), axis=-1)[..., :-1]
    out = jnp.einsum('bnkgqs,bnskd->bnqkgd', probs.astype(v.dtype), vv)
    return out.reshape(B, S, D_ATTN)


def swiglu(h, w_in, w_out):
    gate, up = jnp.split(h @ w_in, 2, axis=-1)
    return (jax.nn.silu(gate) * up) @ w_out


def moe_swiglu(h, w_router, w_in, w_out):
    logits = (h @ w_router).astype(jnp.float32)
    top_v, top_i = lax.top_k(logits, TOP_K)
    probs = jax.nn.softmax(top_v, axis=-1)
    combine = jnp.sum(jax.nn.one_hot(top_i, N_EXPERTS, dtype=jnp.float32) * probs[..., None], axis=-2)
    out = jnp.zeros_like(h)
    for e in range(N_EXPERTS):
        out = out + combine[..., e:e + 1].astype(h.dtype) * swiglu(h, w_in[e], w_out[e])
    return out


def setup_inputs(seed: int = 0) -> dict:
    key = jax.random.key(seed)
    ks = jax.random.split(key, 24)
    f32 = jnp.float32
    L = DEPTH
    nrm = lambda k, shape, scale: jax.random.normal(k, shape, f32) * scale
    u = jax.random.uniform(ks[13], (L, D_RNN), f32, 0.9, 0.999) ** (1.0 / LRU_C)
    return {
        'x': nrm(ks[0], (BATCH, SEQ, D_MODEL), 1.0),
        'c': nrm(ks[1], (BATCH, D_MODEL), 1.0),
        'w_ada': nrm(ks[2], (L, D_MODEL, 6 * D_MODEL), 0.5 * D_MODEL ** -0.5),
        'b_ada': nrm(ks[3], (L, 6 * D_MODEL), 0.01),
        'pre_mix_gain': 1.0 + nrm(ks[4], (L, D_MODEL), 0.02),
        'post_mix_gain': 1.0 + nrm(ks[5], (L, D_MODEL), 0.02),
        'pre_ffn_gain': 1.0 + nrm(ks[6], (L, D_MODEL), 0.02),
        'post_ffn_gain': 1.0 + nrm(ks[7], (L, D_MODEL), 0.02),
        'w_in': nrm(ks[8], (L, D_MODEL, D_IN), D_MODEL ** -0.5),
        'conv_w': nrm(ks[9], (L, CONV_WIDTH, D_RNN), CONV_WIDTH ** -0.5),
        'conv_b': nrm(ks[10], (L, D_RNN), 0.01),
        'w_rg_a': nrm(ks[11], (L, N_RNN_BLOCKS, RNN_BLOCK, RNN_BLOCK), RNN_BLOCK ** -0.5),
        'b_rg_a': nrm(ks[12], (L, D_RNN), 0.01),
        'w_rg_x': nrm(ks[14], (L, N_RNN_BLOCKS, RNN_BLOCK, RNN_BLOCK), RNN_BLOCK ** -0.5),
        'b_rg_x': nrm(ks[15], (L, D_RNN), 0.01),
        'lru_lambda': jnp.log(u) - jnp.log1p(-u),
        'attn_sinks': nrm(ks[16], (L, N_HEADS), 1.0),
        'w_branch': nrm(ks[17], (L, N_BRANCH, D_BRANCH, D_MODEL), D_BRANCH ** -0.5),
        'w_out': nrm(ks[18], (L, D_MODEL, D_MODEL), D_MODEL ** -0.5),
        'w_ffn_in': nrm(ks[19], (N_DENSE, D_MODEL, 2 * D_FF), D_MODEL ** -0.5),
        'w_ffn_out': nrm(ks[20], (N_DENSE, D_FF, D_MODEL), D_FF ** -0.5),
        'w_router': nrm(ks[21], (N_MOE, D_MODEL, N_EXPERTS), D_MODEL ** -0.5),
        'w_moe_in': nrm(ks[22], (N_MOE, N_EXPERTS, D_MODEL, 2 * D_FF_EXPERT), D_MODEL ** -0.5),
        'w_moe_out': nrm(ks[23], (N_MOE, N_EXPERTS, D_FF_EXPERT, D_MODEL), D_FF_EXPERT ** -0.5),
    }


def reference(x, c, w_ada, b_ada, pre_mix_gain, post_mix_gain, pre_ffn_gain, post_ffn_gain,
              w_in, conv_w, conv_b, w_rg_a, b_rg_a, w_rg_x, b_rg_x, lru_lambda, attn_sinks,
              w_branch, w_out, w_ffn_in, w_ffn_out, w_router, w_moe_in, w_moe_out):
    B, S, _ = x.shape
    c_act = jax.nn.silu(c)
    for l in range(DEPTH):
        ada = c_act @ w_ada[l] + b_ada[l]
        sh_m, sc_m, g_m, sh_f, sc_f, g_f = [t[:, None, :] for t in jnp.split(ada, 6, axis=-1)]

        h = rms_norm(x, pre_mix_gain[l]) * (1.0 + sc_m) + sh_m
        proj = h @ w_in[l]
        x_r = proj[..., OFF_XR:OFF_GR]
        g_r = proj[..., OFF_GR:OFF_Q]
        q = proj[..., OFF_Q:OFF_K].reshape(B, S, N_HEADS, HEAD_DIM)
        k = proj[..., OFF_K:OFF_V].reshape(B, S, N_KV_HEADS, HEAD_DIM)
        v = proj[..., OFF_V:OFF_G].reshape(B, S, N_KV_HEADS, HEAD_DIM)
        gates = jax.nn.sigmoid(proj[..., OFF_G:].astype(jnp.float32)).reshape(B, S, N_BRANCH, D_MODEL)

        x_r = causal_depthwise_conv(x_r, conv_w[l], conv_b[l])
        rnn_out = rg_lru(x_r, w_rg_a[l], b_rg_a[l], w_rg_x[l], b_rg_x[l], lru_lambda[l]) * jax.nn.gelu(g_r)
        attn_out = sliding_window_attention(q, k, v, attn_sinks[l])

        branches = jnp.stack([rnn_out, attn_out], axis=2)
        branch_proj = jnp.einsum('bsnk,nkd->bsnd', branches, w_branch[l])
        merged = jnp.sum(gates.astype(branch_proj.dtype) * branch_proj, axis=2)
        mix = merged @ w_out[l]
        x = x + g_m * rms_norm(mix, post_mix_gain[l])

        h = rms_norm(x, pre_ffn_gain[l]) * (1.0 + sc_f) + sh_f
        if l % 2 == 0:
            f = swiglu(h, w_ffn_in[l // 2], w_ffn_out[l // 2])
        else:
            f = moe_swiglu(h, w_router[l // 2], w_moe_in[l // 2], w_moe_out[l // 2])
        x = x + g_f * rms_norm(f, post_ffn_gain[l])
    return x
```

```python
import functools
import math

import jax
import jax.numpy as jnp
from jax import lax
from jax.experimental import pallas as pl
from jax.experimental.pallas import tpu as pltpu

D_MODEL = 1024
D_RNN = 1024
N_RNN_BLOCKS = 8
RNN_BLOCK = D_RNN // N_RNN_BLOCKS
CONV_WIDTH = 4
LRU_C = 8.0
N_HEADS = 8
N_KV_HEADS = 2
GROUP = N_HEADS // N_KV_HEADS
HEAD_DIM = 128
WINDOW = 128
D_ATTN = N_HEADS * HEAD_DIM
D_KV = N_KV_HEADS * HEAD_DIM
D_GATES = 2 * D_MODEL
D_IN = 2 * D_RNN + D_ATTN + 2 * D_KV + D_GATES
D_FF = 3 * D_MODEL
N_EXPERTS = 8
D_FF_EXPERT = D_FF // 2
EPS = 1e-6

BF16 = jnp.bfloat16
F32 = jnp.float32

SUBLANES = 8
LANES = 128
MIB = 1024 * 1024

ROW_TILE = 512
SEQ_TILE = 512
Q_TILE = 512
PROJ_CHUNK = 512
FF_CHUNK = 512
MOE_SUPER = 2048
MOE_SUB = 256
MOE_CHUNK = 256
MASK_VALUE = -1e30


def _params(semantics, vmem_mib):
    return pltpu.CompilerParams(dimension_semantics=semantics, vmem_limit_bytes=vmem_mib * MIB)


def _resident(shape, index_map):
    return pl.BlockSpec(shape, index_map, pipeline_mode=pl.Buffered(1))


def _rms(x, gain):
    return x * lax.rsqrt(jnp.mean(x * x, axis=-1, keepdims=True) + EPS) * gain


def _gelu_tanh(x):
    c = math.sqrt(2.0 / math.pi)
    return 0.5 * x * (1.0 + jnp.tanh(c * (x + 0.044715 * (x * x * x))))


def _silu(x):
    return x * jax.nn.sigmoid(x)


def _ada_kernel(c_ref, w_ref, b_ref, o_ref):
    c = c_ref[...]
    o_ref[...] = jnp.dot(_silu(c), w_ref[...], preferred_element_type=F32,
                         precision=lax.Precision.HIGHEST) + b_ref[...]


def _ada(c, w_ada, b_ada):
    n_layers, d, n6 = w_ada.shape
    b = c.shape[0]
    c_pad = jnp.zeros((SUBLANES, d), F32).at[:b].set(c)
    nb = 1536
    out = pl.pallas_call(
        _ada_kernel,
        grid=(n_layers, n6 // nb),
        in_specs=[
            pl.BlockSpec((SUBLANES, d), lambda l, j: (0, 0)),
            pl.BlockSpec((None, d, nb), lambda l, j: (l, 0, j)),
            pl.BlockSpec((None, 1, nb), lambda l, j: (l, 0, j)),
        ],
        out_specs=pl.BlockSpec((None, SUBLANES, nb), lambda l, j: (l, 0, j)),
        out_shape=jax.ShapeDtypeStruct((n_layers, SUBLANES, n6), F32),
        compiler_params=_params(("arbitrary", "arbitrary"), 32),
        name="ada",
    )(c_pad, w_ada, b_ada.reshape(n_layers, 1, n6))
    return out[:, :b].reshape(n_layers, b, 1, n6)


def _ada_spec(col, tiles_per_batch):
    return pl.BlockSpec((None, 1, D_MODEL), lambda i, *_: (i // tiles_per_batch, 0, col))


def _inproj_kernel(x_ref, gain_ref, sc_ref, sh_ref, w_ref, xr_ref, gr_ref, q_ref, kv_ref, gt_ref):
    h = _rms(x_ref[...], gain_ref[...]) * (1.0 + sc_ref[...]) + sh_ref[...]
    h = h.astype(BF16)
    outs = ((xr_ref, D_RNN), (gr_ref, D_RNN), (q_ref, D_ATTN), (kv_ref, 2 * D_KV), (gt_ref, D_GATES))
    col = 0
    for ref, width in outs:
        for c in range(0, width, PROJ_CHUNK):
            ref[:, c:c + PROJ_CHUNK] = jnp.dot(
                h, w_ref[:, col + c:col + c + PROJ_CHUNK], preferred_element_type=F32).astype(ref.dtype)
        col += width


def _inproj(x2, gain, ada_l, w_in_bf16, tiles_per_batch):
    t = x2.shape[0]
    tm = ROW_TILE
    widths = (D_RNN, D_RNN, D_ATTN, 2 * D_KV, D_GATES)
    return pl.pallas_call(
        _inproj_kernel,
        grid=(t // tm,),
        in_specs=[
            pl.BlockSpec((tm, D_MODEL), lambda i: (i, 0)),
            _resident((1, D_MODEL), lambda i: (0, 0)),
            _ada_spec(1, tiles_per_batch),
            _ada_spec(0, tiles_per_batch),
            _resident((D_MODEL, D_IN), lambda i: (0, 0)),
        ],
        out_specs=[pl.BlockSpec((tm, w), lambda i: (i, 0)) for w in widths],
        out_shape=[jax.ShapeDtypeStruct((t, w), BF16) for w in widths],
        compiler_params=_params(("parallel",), 48),
        name="inproj",
    )(x2, gain, ada_l, ada_l, w_in_bf16)


def _rnn_kernel(xr_ref, gr_ref, cw_ref, cb_ref, wg_ref, ba_ref, bx_ref, lam_ref, o_ref,
                xbuf, abuf, ubuf, hcar):
    ts = xr_ref.shape[0]
    t = pl.program_id(1)

    @pl.when(t == 0)
    def _():
        xbuf[0:SUBLANES, :] = jnp.zeros((SUBLANES, D_RNN), F32)
        hcar[...] = jnp.zeros((SUBLANES, D_RNN), F32)

    xbuf[SUBLANES:SUBLANES + ts, :] = xr_ref[...].astype(F32)
    xc = cb_ref[...] + cw_ref[0:1, :] * xbuf[5:5 + ts, :]
    for k in range(1, CONV_WIDTH):
        xc = xc + cw_ref[k:k + 1, :] * xbuf[5 + k:5 + k + ts, :]
    xbuf[0:SUBLANES, :] = xbuf[ts:ts + SUBLANES, :]

    lam = lam_ref[...]
    sp = jnp.maximum(-lam, 0.0) + jnp.log1p(jnp.exp(-jnp.abs(lam)))
    xcb = xc.astype(BF16)
    for n in range(N_RNN_BLOCKS):
        lo, hi = n * RNN_BLOCK, (n + 1) * RNN_BLOCK
        g = jnp.dot(xcb[:, lo:hi], wg_ref[n], preferred_element_type=F32)
        r = jax.nn.sigmoid(g[:, :RNN_BLOCK] + ba_ref[:, lo:hi])
        i = jax.nn.sigmoid(g[:, RNN_BLOCK:] + bx_ref[:, lo:hi])
        log_a = (-LRU_C) * r * sp[:, lo:hi]
        a = jnp.exp(log_a)
        mult = jnp.sqrt(1.0 - a * a)
        abuf[:, lo:hi] = a
        ubuf[:, lo:hi] = mult * i * xc[:, lo:hi]

    row = lax.broadcasted_iota(jnp.int32, (SUBLANES, D_RNN), 0)

    def group(gi, hprev):
        r0 = pl.multiple_of(gi * SUBLANES, SUBLANES)
        a = abuf[pl.ds(r0, SUBLANES), :]
        b = ubuf[pl.ds(r0, SUBLANES), :]
        for s in (1, 2, 4):
            a_sh = pltpu.roll(a, s, axis=0)
            b_sh = pltpu.roll(b, s, axis=0)
            m = row >= s
            b = jnp.where(m, a * b_sh + b, b)
            a = jnp.where(m, a * a_sh, a)
        h = a * hprev + b
        ubuf[pl.ds(r0, SUBLANES), :] = h
        return jnp.broadcast_to(h[SUBLANES - 1:SUBLANES, :], (SUBLANES, D_RNN))

    hcar[...] = lax.fori_loop(0, ts // SUBLANES, group, hcar[...])
    o_ref[...] = (ubuf[...] * _gelu_tanh(gr_ref[...].astype(F32))).astype(o_ref.dtype)


def _rnn(xr, gr, conv_w, conv_b, w_gate, b_a, b_x, lam, batch, seq):
    ts = SEQ_TILE
    nt = seq // ts
    row = lambda b, t: (b * nt + t, 0)
    const2 = lambda b, t: (0, 0)
    return pl.pallas_call(
        _rnn_kernel,
        grid=(batch, nt),
        in_specs=[
            pl.BlockSpec((ts, D_RNN), row),
            pl.BlockSpec((ts, D_RNN), row),
            _resident((CONV_WIDTH, D_RNN), const2),
            _resident((1, D_RNN), const2),
            _resident((N_RNN_BLOCKS, RNN_BLOCK, 2 * RNN_BLOCK), lambda b, t: (0, 0, 0)),
            _resident((1, D_RNN), const2),
            _resident((1, D_RNN), const2),
            _resident((1, D_RNN), const2),
        ],
        out_specs=pl.BlockSpec((ts, D_RNN), row),
        out_shape=jax.ShapeDtypeStruct((batch * seq, D_RNN), BF16),
        scratch_shapes=[
            pltpu.VMEM((ts + SUBLANES, D_RNN), F32),
            pltpu.VMEM((ts, D_RNN), F32),
            pltpu.VMEM((ts, D_RNN), F32),
            pltpu.VMEM((SUBLANES, D_RNN), F32),
        ],
        compiler_params=_params(("arbitrary", "arbitrary"), 40),
        name="rnn",
    )(xr, gr, conv_w, conv_b, w_gate, b_a, b_x, lam)


def _attn_bias():
    qi = jnp.arange(WINDOW)[:, None]
    sj = jnp.arange(2 * WINDOW)[None, :]
    dist = (qi + WINDOW - sj).astype(F32)
    valid = (dist >= 0) & (dist < WINDOW)
    slopes = jnp.asarray([2.0 ** (-8.0 * (h + 1) / N_HEADS) for h in range(N_HEADS)], F32)
    bias = jnp.where(valid[None], -slopes[:, None, None] * dist[None], MASK_VALUE)
    return bias.reshape(N_KV_HEADS, GROUP * WINDOW, 2 * WINDOW)


def _attn_kernel(sink_ref, q_ref, kv_ref, kvp_ref, bias_ref, o_ref):
    tq = q_ref.shape[0]
    first = pl.program_id(1) == 0
    col = lax.broadcasted_iota(jnp.int32, (GROUP * WINDOW, 2 * WINDOW), 1)
    first_mask = jnp.where(first & (col < WINDOW), MASK_VALUE, 0.0).astype(F32)
    scale = HEAD_DIM ** -0.5
    for jb in range(tq // WINDOW):
        r0 = jb * WINDOW
        for kvh in range(N_KV_HEADS):
            kc = slice(kvh * HEAD_DIM, (kvh + 1) * HEAD_DIM)
            vc = slice(D_KV + kvh * HEAD_DIM, D_KV + (kvh + 1) * HEAD_DIM)
            if jb == 0:
                k_prev, v_prev = kvp_ref[:, kc], kvp_ref[:, vc]
            else:
                k_prev, v_prev = kv_ref[r0 - WINDOW:r0, kc], kv_ref[r0 - WINDOW:r0, vc]
            k = jnp.concatenate([k_prev, kv_ref[r0:r0 + WINDOW, kc]], axis=0)
            v = jnp.concatenate([v_prev, kv_ref[r0:r0 + WINDOW, vc]], axis=0)
            heads = [kvh * GROUP + g for g in range(GROUP)]
            q = jnp.concatenate(
                [q_ref[r0:r0 + WINDOW, h * HEAD_DIM:(h + 1) * HEAD_DIM] for h in heads], axis=0)
            s = lax.dot_general(q, k, (((1,), (1,)), ((), ())), preferred_element_type=F32)
            s = s * scale + bias_ref[kvh]
            if jb == 0:
                s = s + first_mask
            sink = jnp.concatenate(
                [jnp.full((WINDOW, 1), sink_ref[h], F32) for h in heads], axis=0)
            m = jnp.maximum(jnp.max(s, axis=-1, keepdims=True), sink)
            p = jnp.exp(s - m)
            denom = jnp.sum(p, axis=-1, keepdims=True) + jnp.exp(sink - m)
            o = jnp.dot(p.astype(BF16), v, preferred_element_type=F32) / denom
            for g, h in enumerate(heads):
                o_ref[r0:r0 + WINDOW, h * HEAD_DIM:(h + 1) * HEAD_DIM] = (
                    o[g * WINDOW:(g + 1) * WINDOW, :].astype(o_ref.dtype))


def _attn(q, kv, sinks, batch, seq):
    tq = Q_TILE
    nq = seq // tq
    blocks_per_tile = tq // WINDOW
    blocks_per_seq = seq // WINDOW
    row = lambda b, i: (b * nq + i, 0)
    prev = lambda b, i: (b * blocks_per_seq + jnp.maximum(i * blocks_per_tile - 1, 0), 0)
    return pl.pallas_call(
        _attn_kernel,
        grid=(batch, nq),
        in_specs=[
            pl.BlockSpec(memory_space=pltpu.SMEM),
            pl.BlockSpec((tq, D_ATTN), row),
            pl.BlockSpec((tq, 2 * D_KV), row),
            pl.BlockSpec((WINDOW, 2 * D_KV), prev),
            _resident((N_KV_HEADS, GROUP * WINDOW, 2 * WINDOW), lambda b, i: (0, 0, 0)),
        ],
        out_specs=pl.BlockSpec((tq, D_ATTN), row),
        out_shape=jax.ShapeDtypeStruct((batch * seq, D_ATTN), BF16),
        compiler_params=_params(("parallel", "parallel"), 32),
        name="attn",
    )(sinks, q, kv, kv, _attn_bias())


def _mix_kernel(rnn_ref, att_ref, gt_ref, x_ref, gm_ref, gain_ref, wb_ref, wo_ref, o_ref):
    bp0 = jnp.dot(rnn_ref[...], wb_ref[0], preferred_element_type=F32)
    bp1 = jnp.dot(att_ref[...], wb_ref[1], preferred_element_type=F32)
    g0 = jax.nn.sigmoid(gt_ref[:, :D_MODEL].astype(F32))
    g1 = jax.nn.sigmoid(gt_ref[:, D_MODEL:].astype(F32))
    merged = (g0 * bp0 + g1 * bp1).astype(BF16)
    mix = jnp.dot(merged, wo_ref[...], preferred_element_type=F32)
    o_ref[...] = x_ref[...] + gm_ref[...] * _rms(mix, gain_ref[...])


def _mix(rnn_out, attn_out, gates, x2, ada_l, gain, w_branch, w_out, tiles_per_batch):
    t = x2.shape[0]
    tm = ROW_TILE
    row = lambda i: (i, 0)
    return pl.pallas_call(
        _mix_kernel,
        grid=(t // tm,),
        in_specs=[
            pl.BlockSpec((tm, D_MODEL), row),
            pl.BlockSpec((tm, D_MODEL), row),
            pl.BlockSpec((tm, D_GATES), row),
            pl.BlockSpec((tm, D_MODEL), row),
            _ada_spec(2, tiles_per_batch),
            _resident((1, D_MODEL), lambda i: (0, 0)),
            _resident((2, D_MODEL, D_MODEL), lambda i: (0, 0, 0)),
            _resident((D_MODEL, D_MODEL), lambda i: (0, 0)),
        ],
        out_specs=pl.BlockSpec((tm, D_MODEL), row),
        out_shape=jax.ShapeDtypeStruct((t, D_MODEL), F32),
        compiler_params=_params(("parallel",), 40),
        name="mix",
    )(rnn_out, attn_out, gates, x2, ada_l, gain, w_branch, w_out)


def _ffn_kernel(x_ref, g1_ref, sc_ref, sh_ref, gf_ref, g2_ref, wi_ref, wo_ref, o_ref, acc):
    x = x_ref[...]
    h = (_rms(x, g1_ref[...]) * (1.0 + sc_ref[...]) + sh_ref[...]).astype(BF16)
    for c in range(0, D_FF, FF_CHUNK):
        gate = jnp.dot(h, wi_ref[:, c:c + FF_CHUNK], preferred_element_type=F32)
        up = jnp.dot(h, wi_ref[:, D_FF + c:D_FF + c + FF_CHUNK], preferred_element_type=F32)
        a = (_silu(gate) * up).astype(BF16)
        part = jnp.dot(a, wo_ref[c:c + FF_CHUNK, :], preferred_element_type=F32)
        if c == 0:
            acc[...] = part
        else:
            acc[...] += part
    o_ref[...] = x + gf_ref[...] * _rms(acc[...], g2_ref[...])


def _ffn(x2, ada_l, gain1, gain2, w_in, w_out, tiles_per_batch):
    t = x2.shape[0]
    tm = ROW_TILE
    row = lambda i: (i, 0)
    return pl.pallas_call(
        _ffn_kernel,
        grid=(t // tm,),
        in_specs=[
            pl.BlockSpec((tm, D_MODEL), row),
            _resident((1, D_MODEL), lambda i: (0, 0)),
            _ada_spec(4, tiles_per_batch),
            _ada_spec(3, tiles_per_batch),
            _ada_spec(5, tiles_per_batch),
            _resident((1, D_MODEL), lambda i: (0, 0)),
            _resident((D_MODEL, 2 * D_FF), lambda i: (0, 0)),
            _resident((D_FF, D_MODEL), lambda i: (0, 0)),
        ],
        out_specs=pl.BlockSpec((tm, D_MODEL), row),
        out_shape=jax.ShapeDtypeStruct((t, D_MODEL), F32),
        scratch_shapes=[pltpu.VMEM((tm, D_MODEL), F32)],
        compiler_params=_params(("parallel",), 48),
        name="ffn",
    )(x2, gain1, ada_l, ada_l, ada_l, gain2, w_in, w_out)


def _router_kernel(x_ref, g1_ref, sc_ref, sh_ref, wr_ref, h_ref, comb_ref):
    h = _rms(x_ref[...], g1_ref[...]) * (1.0 + sc_ref[...]) + sh_ref[...]
    h_ref[...] = h.astype(h_ref.dtype)
    logits = jnp.dot(h, wr_ref[...], preferred_element_type=F32, precision=lax.Precision.HIGHEST)
    lane = lax.broadcasted_iota(jnp.int32, logits.shape, 1)
    neg = jnp.float32(-jnp.inf)
    lg = jnp.where(lane < N_EXPERTS, logits, neg)
    m1 = jnp.max(lg, axis=-1, keepdims=True)
    i1 = jnp.min(jnp.where(lg == m1, lane, LANES), axis=-1, keepdims=True)
    lg2 = jnp.where(lane == i1, neg, lg)
    m2 = jnp.max(lg2, axis=-1, keepdims=True)
    i2 = jnp.min(jnp.where(lg2 == m2, lane, LANES), axis=-1, keepdims=True)
    e2 = jnp.exp(m2 - m1)
    p1 = 1.0 / (1.0 + e2)
    p2 = e2 / (1.0 + e2)
    comb_ref[...] = jnp.where(lane == i1, p1, 0.0) + jnp.where(lane == i2, p2, 0.0)


def _router(x2, ada_l, gain1, w_router_pad, tiles_per_batch):
    t = x2.shape[0]
    tm = ROW_TILE
    row = lambda i: (i, 0)
    return pl.pallas_call(
        _router_kernel,
        grid=(t // tm,),
        in_specs=[
            pl.BlockSpec((tm, D_MODEL), row),
            _resident((1, D_MODEL), lambda i: (0, 0)),
            _ada_spec(4, tiles_per_batch),
            _ada_spec(3, tiles_per_batch),
            _resident((D_MODEL, LANES), lambda i: (0, 0)),
        ],
        out_specs=[pl.BlockSpec((tm, D_MODEL), row), pl.BlockSpec((tm, LANES), row)],
        out_shape=[jax.ShapeDtypeStruct((t, D_MODEL), BF16), jax.ShapeDtypeStruct((t, LANES), F32)],
        compiler_params=_params(("parallel",), 32),
        name="router",
    )(x2, gain1, ada_l, ada_l, w_router_pad)


def _moe_kernel(nch_ref, cum_ref, h_ref, srow_ref, scol_ref, comb_ref, wi_ref, wo_ref, o_ref, xg, ybuf):
    s = pl.program_id(0)
    e = pl.program_id(1)
    n_sub = MOE_SUPER // MOE_SUB
    group = s * N_EXPERTS + e

    @pl.when(e == 0)
    def _():
        o_ref[...] = jnp.zeros(o_ref.shape, F32)

    lane = lax.broadcasted_iota(jnp.int32, (MOE_SUB, LANES), 1)
    dest_col = lax.broadcasted_iota(jnp.int32, (MOE_CHUNK, MOE_SUB), 0)
    dest_row = lax.broadcasted_iota(jnp.int32, (MOE_SUB, MOE_CHUNK), 1)

    def chunk(c, carry):
        base = c * MOE_CHUNK

        def overlaps(j):
            lo = cum_ref[group * (n_sub + 1) + j]
            hi = cum_ref[group * (n_sub + 1) + j + 1]
            return (lo < base + MOE_CHUNK) & (hi > base)

        xg[...] = jnp.zeros(xg.shape, F32)
        for j in range(n_sub):
            @pl.when(overlaps(j))
            def _():
                srow = srow_ref[:, j * MOE_SUB:(j + 1) * MOE_SUB]
                onehot = jnp.where(srow - base == dest_col, 1.0, 0.0).astype(BF16)
                xg[...] += jnp.dot(onehot, h_ref[j * MOE_SUB:(j + 1) * MOE_SUB, :],
                                   preferred_element_type=F32)

        x_e = xg[...].astype(BF16)
        gate = jnp.dot(x_e, wi_ref[:, :D_FF_EXPERT], preferred_element_type=F32)
        up = jnp.dot(x_e, wi_ref[:, D_FF_EXPERT:], preferred_element_type=F32)
        a = (_silu(gate) * up).astype(BF16)
        ybuf[...] = jnp.dot(a, wo_ref[...], preferred_element_type=F32).astype(BF16)

        for j in range(n_sub):
            @pl.when(overlaps(j))
            def _():
                rows = slice(j * MOE_SUB, (j + 1) * MOE_SUB)
                sel = lane == e
                scol = jnp.sum(jnp.where(sel, scol_ref[rows, :], 0.0), axis=-1, keepdims=True)
                prob = jnp.sum(jnp.where(sel, comb_ref[rows, :], 0.0), axis=-1, keepdims=True)
                onehot = jnp.where(scol.astype(jnp.int32) - base == dest_row, 1.0, 0.0).astype(BF16)
                o_ref[rows, :] += prob * jnp.dot(onehot, ybuf[...], preferred_element_type=F32)

        return carry

    lax.fori_loop(0, nch_ref[group], chunk, 0)


def _moe(h, comb, w_in, w_out):
    t = h.shape[0]
    n_super = t // MOE_SUPER
    n_sub = MOE_SUPER // MOE_SUB
    sel = (comb[:, :N_EXPERTS] > 0.0).reshape(n_super, MOE_SUPER, N_EXPERTS)
    seli = sel.astype(jnp.int32)
    slot = jnp.where(sel, jnp.cumsum(seli, axis=1) - 1, -1)
    slot_rows = slot.transpose(0, 2, 1).reshape(n_super, N_EXPERTS, 1, MOE_SUPER)
    slot_cols = jnp.pad(slot.reshape(t, N_EXPERTS).astype(F32), ((0, 0), (0, LANES - N_EXPERTS)),
                        constant_values=-1.0)
    counts = seli.sum(axis=1)
    n_chunks = ((counts + MOE_CHUNK - 1) // MOE_CHUNK).reshape(-1).astype(jnp.int32)
    sub_counts = seli.reshape(n_super, n_sub, MOE_SUB, N_EXPERTS).sum(axis=2)
    cum = jnp.concatenate([jnp.zeros((n_super, 1, N_EXPERTS), jnp.int32),
                           jnp.cumsum(sub_counts, axis=1)], axis=1)
    cum = cum.transpose(0, 2, 1).reshape(-1).astype(jnp.int32)

    grid_spec = pltpu.PrefetchScalarGridSpec(
        num_scalar_prefetch=2,
        grid=(n_super, N_EXPERTS),
        in_specs=[
            pl.BlockSpec((MOE_SUPER, D_MODEL), lambda s, e, *_: (s, 0)),
            pl.BlockSpec((None, None, 1, MOE_SUPER), lambda s, e, *_: (s, e, 0, 0)),
            pl.BlockSpec((MOE_SUPER, LANES), lambda s, e, *_: (s, 0)),
            pl.BlockSpec((MOE_SUPER, LANES), lambda s, e, *_: (s, 0)),
            pl.BlockSpec((None, D_MODEL, 2 * D_FF_EXPERT), lambda s, e, *_: (e, 0, 0)),
            pl.BlockSpec((None, D_FF_EXPERT, D_MODEL), lambda s, e, *_: (e, 0, 0)),
        ],
        out_specs=pl.BlockSpec((MOE_SUPER, D_MODEL), lambda s, e, *_: (s, 0)),
        scratch_shapes=[pltpu.VMEM((MOE_CHUNK, D_MODEL), F32), pltpu.VMEM((MOE_CHUNK, D_MODEL), BF16)],
    )
    return pl.pallas_call(
        _moe_kernel,
        grid_spec=grid_spec,
        out_shape=jax.ShapeDtypeStruct((t, D_MODEL), F32),
        compiler_params=_params(("arbitrary", "arbitrary"), 58),
        name="moe",
    )(n_chunks, cum, h, slot_rows, slot_cols, comb, w_in, w_out)


def _resid_kernel(x_ref, f_ref, gf_ref, g2_ref, o_ref):
    o_ref[...] = x_ref[...] + gf_ref[...] * _rms(f_ref[...], g2_ref[...])


def _resid(x2, f, ada_l, gain2, tiles_per_batch):
    t = x2.shape[0]
    tm = ROW_TILE
    row = lambda i: (i, 0)
    return pl.pallas_call(
        _resid_kernel,
        grid=(t // tm,),
        in_specs=[
            pl.BlockSpec((tm, D_MODEL), row),
            pl.BlockSpec((tm, D_MODEL), row),
            _ada_spec(5, tiles_per_batch),
            _resident((1, D_MODEL), lambda i: (0, 0)),
        ],
        out_specs=pl.BlockSpec((tm, D_MODEL), row),
        out_shape=jax.ShapeDtypeStruct((t, D_MODEL), F32),
        compiler_params=_params(("parallel",), 32),
        name="resid",
    )(x2, f, ada_l, gain2)


def kernel(x, c, w_ada, b_ada, pre_mix_gain, post_mix_gain, pre_ffn_gain, post_ffn_gain, w_in, conv_w, conv_b,
           w_rg_a, b_rg_a, w_rg_x, b_rg_x, lru_lambda, attn_sinks, w_branch, w_out, w_ffn_in, w_ffn_out,
           w_router, w_moe_in, w_moe_out):
    batch, seq, d = x.shape
    depth = w_in.shape[0]
    t = batch * seq
    tiles_per_batch = seq // ROW_TILE
    row1 = lambda v: v.reshape(1, -1)

    ada = _ada(c, w_ada, b_ada)
    x2 = x.reshape(t, d)
    for l in range(depth):
        ada_l = ada[l]
        xr, gr, q, kv, gates = _inproj(x2, row1(pre_mix_gain[l]), ada_l, w_in[l].astype(BF16), tiles_per_batch)
        w_gate = jnp.concatenate([w_rg_a[l], w_rg_x[l]], axis=-1).astype(BF16)
        rnn_out = _rnn(xr, gr, conv_w[l], row1(conv_b[l]), w_gate, row1(b_rg_a[l]), row1(b_rg_x[l]),
                       row1(lru_lambda[l]), batch, seq)
        attn_out = _attn(q, kv, attn_sinks[l], batch, seq)
        x2 = _mix(rnn_out, attn_out, gates, x2, ada_l, row1(post_mix_gain[l]), w_branch[l].astype(BF16),
                  w_out[l].astype(BF16), tiles_per_batch)
        if l % 2 == 0:
            x2 = _ffn(x2, ada_l, row1(pre_ffn_gain[l]), row1(post_ffn_gain[l]), w_ffn_in[l // 2].astype(BF16),
                      w_ffn_out[l // 2].astype(BF16), tiles_per_batch)
        else:
            w_r = jnp.pad(w_router[l // 2], ((0, 0), (0, LANES - N_EXPERTS)))
            h, comb = _router(x2, ada_l, row1(pre_ffn_gain[l]), w_r, tiles_per_batch)
            f = _moe(h, comb, w_moe_in[l // 2].astype(BF16), w_moe_out[l // 2].astype(BF16))
            x2 = _resid(x2, f, ada_l, row1(post_ffn_gain[l]), tiles_per_batch)
    return x2.reshape(batch, seq, d)
```

```python
import functools
import math

import jax
import jax.numpy as jnp
from jax import lax
from jax.experimental import pallas as pl
from jax.experimental.pallas import tpu as pltpu

D_MODEL = 1024
D_RNN = 1024
N_RNN_BLOCKS = 8
RNN_BLOCK = D_RNN // N_RNN_BLOCKS
CONV_WIDTH = 4
LRU_C = 8.0
N_HEADS = 8
N_KV_HEADS = 2
GROUP = N_HEADS // N_KV_HEADS
HEAD_DIM = 128
WINDOW = 128
D_ATTN = N_HEADS * HEAD_DIM
D_KV = N_KV_HEADS * HEAD_DIM
D_GATES = 2 * D_MODEL
D_IN = 2 * D_RNN + D_ATTN + 2 * D_KV + D_GATES
D_FF = 3 * D_MODEL
N_EXPERTS = 8
D_FF_EXPERT = D_FF // 2
EPS = 1e-6

BF16 = jnp.bfloat16
F32 = jnp.float32

SUBLANES = 8
LANES = 128
MIB = 1024 * 1024

ROW_TILE = 512
SEQ_TILE = 512
Q_TILE = 512
PROJ_CHUNK = 512
FF_CHUNK = 512
MOE_SUPER = 2048
MOE_SUB = 256
MOE_PART = 128
MOE_ALIGN = 16
MOE_CHUNK = 256
MASK_VALUE = -1e30
LOG2E = math.log2(math.e)


def _params(semantics, vmem_mib):
    return pltpu.CompilerParams(dimension_semantics=semantics, vmem_limit_bytes=vmem_mib * MIB)


def _resident(shape, index_map):
    return pl.BlockSpec(shape, index_map, pipeline_mode=pl.Buffered(1))


def _rms(x, gain):
    return x * lax.rsqrt(jnp.mean(x * x, axis=-1, keepdims=True) + EPS) * gain


def _gelu_tanh(x):
    c = math.sqrt(2.0 / math.pi)
    return 0.5 * x * (1.0 + jnp.tanh(c * (x + 0.044715 * (x * x * x))))


def _silu(x):
    return x * jax.nn.sigmoid(x)


def _ada_kernel(c_ref, w_ref, b_ref, o_ref):
    c = c_ref[...]
    o_ref[...] = jnp.dot(_silu(c), w_ref[...], preferred_element_type=F32,
                         precision=lax.Precision.HIGHEST) + b_ref[...]


def _ada(c, w_ada, b_ada):
    n_layers, d, n6 = w_ada.shape
    b = c.shape[0]
    c_pad = jnp.zeros((SUBLANES, d), F32).at[:b].set(c)
    nb = 1536
    out = pl.pallas_call(
        _ada_kernel,
        grid=(n_layers, n6 // nb),
        in_specs=[
            pl.BlockSpec((SUBLANES, d), lambda l, j: (0, 0)),
            pl.BlockSpec((None, d, nb), lambda l, j: (l, 0, j)),
            pl.BlockSpec((None, 1, nb), lambda l, j: (l, 0, j)),
        ],
        out_specs=pl.BlockSpec((None, SUBLANES, nb), lambda l, j: (l, 0, j)),
        out_shape=jax.ShapeDtypeStruct((n_layers, SUBLANES, n6), F32),
        compiler_params=_params(("arbitrary", "arbitrary"), 32),
        name="ada",
    )(c_pad, w_ada, b_ada.reshape(n_layers, 1, n6))
    return out[:, :b].reshape(n_layers, b, 1, n6)


def _ada_spec(col, tiles_per_batch):
    return pl.BlockSpec((None, 1, D_MODEL), lambda i, *_: (i // tiles_per_batch, 0, col))


def _inproj_kernel(x_ref, gain_ref, sc_ref, sh_ref, w_ref, xr_ref, gr_ref, q_ref, kv_ref, gt_ref):
    h = _rms(x_ref[...], gain_ref[...]) * (1.0 + sc_ref[...]) + sh_ref[...]
    h = h.astype(BF16)
    outs = ((xr_ref, D_RNN), (gr_ref, D_RNN), (q_ref, D_ATTN), (kv_ref, 2 * D_KV), (gt_ref, D_GATES))
    col = 0
    for ref, width in outs:
        for c in range(0, width, PROJ_CHUNK):
            ref[:, c:c + PROJ_CHUNK] = jnp.dot(
                h, w_ref[:, col + c:col + c + PROJ_CHUNK], preferred_element_type=F32).astype(ref.dtype)
        col += width


def _inproj(x2, gain, ada_l, w_in_bf16, tiles_per_batch):
    t = x2.shape[0]
    tm = ROW_TILE
    widths = (D_RNN, D_RNN, D_ATTN, 2 * D_KV, D_GATES)
    return pl.pallas_call(
        _inproj_kernel,
        grid=(t // tm,),
        in_specs=[
            pl.BlockSpec((tm, D_MODEL), lambda i: (i, 0)),
            _resident((1, D_MODEL), lambda i: (0, 0)),
            _ada_spec(1, tiles_per_batch),
            _ada_spec(0, tiles_per_batch),
            _resident((D_MODEL, D_IN), lambda i: (0, 0)),
        ],
        out_specs=[pl.BlockSpec((tm, w), lambda i: (i, 0)) for w in widths],
        out_shape=[jax.ShapeDtypeStruct((t, w), BF16) for w in widths],
        compiler_params=_params(("parallel",), 48),
        name="inproj",
    )(x2, gain, ada_l, ada_l, w_in_bf16)


def _rnn_kernel(xr_ref, gr_ref, perm_ref, unperm_ref, cw_ref, cb_ref, wg_ref, ba_ref, bx_ref, lam_ref, o_ref,
                xs, gs, outp, hist, hcar):
    ts = xr_ref.shape[0]
    seg = ts // SUBLANES
    t = pl.program_id(1)

    @pl.when(t == 0)
    def _():
        hist[...] = jnp.zeros((SUBLANES, D_RNN), F32)
        hcar[...] = jnp.zeros((SUBLANES, D_RNN), F32)

    sub = lax.broadcasted_iota(jnp.int32, (SUBLANES, RNN_BLOCK), 0)
    xs[...] = jnp.dot(perm_ref[...], xr_ref[...], preferred_element_type=F32)
    gs[...] = jnp.dot(perm_ref[...], gr_ref[...], preferred_element_type=F32)
    lam = lam_ref[...]
    sp2 = (-LRU_C * LOG2E) * (jnp.maximum(-lam, 0.0) + jnp.log1p(jnp.exp(-jnp.abs(lam))))
    c = math.sqrt(2.0 / math.pi)
    vrow = lambda arr, v: arr[v * SUBLANES:(v + 1) * SUBLANES, :]

    for n in range(N_RNN_BLOCKS):
        cols = slice(n * RNN_BLOCK, (n + 1) * RNN_BLOCK)
        xp = [xs[v * SUBLANES:(v + 1) * SUBLANES, cols] for v in range(seg)]

        def tail(j):
            rolled = pltpu.roll(xp[seg - j], 1, axis=0)
            return jnp.where(sub == 0, jnp.broadcast_to(hist[j:j + 1, cols], (SUBLANES, RNN_BLOCK)), rolled)

        tails = {j: tail(j) for j in range(1, CONV_WIDTH)}
        for j in range(1, CONV_WIDTH):
            hist[j:j + 1, cols] = xp[seg - j][SUBLANES - 1:SUBLANES, :]

        xc = cb_ref[:, cols] + cw_ref[CONV_WIDTH - 1:CONV_WIDTH, cols] * jnp.concatenate(xp, axis=0)
        for j in range(1, CONV_WIDTH):
            k = CONV_WIDTH - 1 - j
            shifted = jnp.concatenate([tails[j - v] for v in range(j)] + xp[:seg - j], axis=0)
            xc = xc + cw_ref[k:k + 1, cols] * shifted

        g = jnp.dot(xc.astype(BF16), wg_ref[n], preferred_element_type=F32)
        r = jax.nn.sigmoid(g[:, :RNN_BLOCK] + ba_ref[:, cols])
        i = jax.nn.sigmoid(g[:, RNN_BLOCK:] + bx_ref[:, cols])
        a = jnp.exp2(r * sp2[:, cols])
        w = 1.0 - a * a
        u = (w * lax.rsqrt(jnp.maximum(w, 1e-30))) * i * xc

        h = jnp.zeros((SUBLANES, RNN_BLOCK), F32)
        prod = jnp.ones((SUBLANES, RNN_BLOCK), F32)
        hs, prods = [], []
        for v in range(seg):
            h = vrow(a, v) * h + vrow(u, v)
            prod = vrow(a, v) * prod
            hs.append(h)
            prods.append(prod)

        pa, pb = prod, h
        for s in (1, 2, 4):
            a_sh = pltpu.roll(pa, s, axis=0)
            b_sh = pltpu.roll(pb, s, axis=0)
            m = sub >= s
            pb = jnp.where(m, pa * b_sh + pb, pb)
            pa = jnp.where(m, pa * a_sh, pa)
        h_end = pb + pa * hcar[:, cols]
        carry_in = jnp.where(sub == 0, hcar[:, cols], pltpu.roll(h_end, 1, axis=0))
        hcar[:, cols] = jnp.broadcast_to(h_end[SUBLANES - 1:SUBLANES, :], (SUBLANES, RNN_BLOCK))

        h_all = jnp.concatenate([hs[v] + prods[v] * carry_in for v in range(seg)], axis=0)

        gg = gs[:, cols]
        th = jnp.tanh(gg * (c + (c * 0.044715) * (gg * gg)))
        hg = h_all * (0.5 * gg)
        outp[:, cols] = (hg + hg * th).astype(BF16)

    o_ref[...] = jnp.dot(unperm_ref[...], outp[...], preferred_element_type=F32).astype(o_ref.dtype)


def _segment_permutation(ts):
    seg = ts // SUBLANES
    p = jnp.arange(ts)
    src = (p % SUBLANES) * seg + p // SUBLANES
    perm = (src[:, None] == jnp.arange(ts)[None, :]).astype(BF16)
    return perm, perm.T


def _rnn(xr, gr, conv_w, conv_b, w_gate, b_a, b_x, lam, batch, seq):
    ts = SEQ_TILE
    nt = seq // ts
    row = lambda b, t: (b * nt + t, 0)
    const2 = lambda b, t: (0, 0)
    perm, unperm = _segment_permutation(ts)
    return pl.pallas_call(
        _rnn_kernel,
        grid=(batch, nt),
        in_specs=[
            pl.BlockSpec((ts, D_RNN), row),
            pl.BlockSpec((ts, D_RNN), row),
            _resident((ts, ts), const2),
            _resident((ts, ts), const2),
            _resident((CONV_WIDTH, D_RNN), const2),
            _resident((1, D_RNN), const2),
            _resident((N_RNN_BLOCKS, RNN_BLOCK, 2 * RNN_BLOCK), lambda b, t: (0, 0, 0)),
            _resident((1, D_RNN), const2),
            _resident((1, D_RNN), const2),
            _resident((1, D_RNN), const2),
        ],
        out_specs=pl.BlockSpec((ts, D_RNN), row),
        out_shape=jax.ShapeDtypeStruct((batch * seq, D_RNN), BF16),
        scratch_shapes=[
            pltpu.VMEM((ts, D_RNN), F32),
            pltpu.VMEM((ts, D_RNN), F32),
            pltpu.VMEM((ts, D_RNN), BF16),
            pltpu.VMEM((SUBLANES, D_RNN), F32),
            pltpu.VMEM((SUBLANES, D_RNN), F32),
        ],
        compiler_params=_params(("arbitrary", "arbitrary"), 40),
        name="rnn",
    )(xr, gr, perm, unperm, conv_w, conv_b, w_gate, b_a, b_x, lam)


def _attn_bias(sinks):
    qi = jnp.arange(WINDOW)[:, None]
    sj = jnp.arange(2 * WINDOW)[None, :]
    dist = (qi + WINDOW - sj).astype(F32)
    valid = (dist >= 0) & (dist < WINDOW)
    slopes = jnp.asarray([2.0 ** (-8.0 * (h + 1) / N_HEADS) for h in range(N_HEADS)], F32)
    bias = jnp.where(valid[None], -slopes[:, None, None] * dist[None], MASK_VALUE)
    bias = jnp.where((sj == 0)[None], sinks.astype(F32)[:, None, None], bias)
    return (bias * LOG2E).reshape(N_KV_HEADS, GROUP * WINDOW, 2 * WINDOW)


def _attn_kernel(q_ref, kv_ref, kvp_ref, bias_ref, o_ref, s_scr, p_scr):
    tq = q_ref.shape[0]
    first = pl.program_id(1) == 0
    col = lax.broadcasted_iota(jnp.int32, (GROUP * WINDOW, 2 * WINDOW), 1)
    first_mask = jnp.where(first & (col >= 1) & (col < WINDOW), MASK_VALUE, 0.0).astype(F32)
    slot0 = lax.broadcasted_iota(jnp.int32, (2 * WINDOW, HEAD_DIM), 0) == 0
    zeros = jnp.zeros((2 * WINDOW, HEAD_DIM), BF16)
    ones = jnp.ones((2 * WINDOW, HEAD_DIM), BF16)
    scale2 = (HEAD_DIM ** -0.5) * LOG2E
    units = [(jb, kvh) for jb in range(tq // WINDOW) for kvh in range(N_KV_HEADS)]

    def keys_or_values(jb, cols):
        r0 = jb * WINDOW
        prev = kvp_ref[:, cols] if jb == 0 else kv_ref[r0 - WINDOW:r0, cols]
        kv = jnp.concatenate([prev, kv_ref[r0:r0 + WINDOW, cols]], axis=0)
        return jnp.where(slot0, zeros, kv)

    for u, (jb, kvh) in enumerate(units):
        r0 = jb * WINDOW
        k = keys_or_values(jb, slice(kvh * HEAD_DIM, (kvh + 1) * HEAD_DIM))
        q = jnp.concatenate(
            [q_ref[r0:r0 + WINDOW, (kvh * GROUP + g) * HEAD_DIM:(kvh * GROUP + g + 1) * HEAD_DIM]
             for g in range(GROUP)], axis=0)
        s = lax.dot_general(q, k, (((1,), (1,)), ((), ())), preferred_element_type=F32)
        s = s * scale2 + bias_ref[kvh]
        if jb == 0:
            s = s + first_mask
        s_scr[u] = s

    for u, (jb, kvh) in enumerate(units):
        r0 = jb * WINDOW
        v = keys_or_values(jb, slice(D_KV + kvh * HEAD_DIM, D_KV + (kvh + 1) * HEAD_DIM))
        v_ext = jnp.concatenate([v, ones], axis=1)
        for g in range(GROUP):
            rows = slice(g * WINDOW, (g + 1) * WINDOW)
            s = s_scr[u, rows, :]
            m = jnp.max(s, axis=-1, keepdims=True)
            p_scr[u, rows, :] = jnp.exp2(s - m).astype(BF16)
        o_ext = jnp.dot(p_scr[u], v_ext, preferred_element_type=F32)
        o = o_ext[:, :HEAD_DIM] / o_ext[:, HEAD_DIM:]
        for g in range(GROUP):
            h = kvh * GROUP + g
            o_ref[r0:r0 + WINDOW, h * HEAD_DIM:(h + 1) * HEAD_DIM] = (
                o[g * WINDOW:(g + 1) * WINDOW, :].astype(o_ref.dtype))


def _attn(q, kv, sinks, batch, seq):
    tq = Q_TILE
    nq = seq // tq
    blocks_per_tile = tq // WINDOW
    blocks_per_seq = seq // WINDOW
    row = lambda b, i: (b * nq + i, 0)
    prev = lambda b, i: (b * blocks_per_seq + jnp.maximum(i * blocks_per_tile - 1, 0), 0)
    return pl.pallas_call(
        _attn_kernel,
        grid=(batch, nq),
        in_specs=[
            pl.BlockSpec((tq, D_ATTN), row),
            pl.BlockSpec((tq, 2 * D_KV), row),
            pl.BlockSpec((WINDOW, 2 * D_KV), prev),
            _resident((N_KV_HEADS, GROUP * WINDOW, 2 * WINDOW), lambda b, i: (0, 0, 0)),
        ],
        out_specs=pl.BlockSpec((tq, D_ATTN), row),
        out_shape=jax.ShapeDtypeStruct((batch * seq, D_ATTN), BF16),
        scratch_shapes=[pltpu.VMEM((blocks_per_tile * N_KV_HEADS, GROUP * WINDOW, 2 * WINDOW), F32),
                        pltpu.VMEM((blocks_per_tile * N_KV_HEADS, GROUP * WINDOW, 2 * WINDOW), BF16)],
        compiler_params=_params(("parallel", "parallel"), 32),
        name="attn",
    )(q, kv, kv, _attn_bias(sinks))


def _mix_kernel(rnn_ref, att_ref, gt_ref, x_ref, gm_ref, gain_ref, wb_ref, wo_ref, o_ref):
    bp0 = jnp.dot(rnn_ref[...], wb_ref[0], preferred_element_type=F32)
    bp1 = jnp.dot(att_ref[...], wb_ref[1], preferred_element_type=F32)
    g0 = jax.nn.sigmoid(gt_ref[:, :D_MODEL].astype(F32))
    g1 = jax.nn.sigmoid(gt_ref[:, D_MODEL:].astype(F32))
    merged = (g0 * bp0 + g1 * bp1).astype(BF16)
    mix = jnp.dot(merged, wo_ref[...], preferred_element_type=F32)
    o_ref[...] = x_ref[...] + gm_ref[...] * _rms(mix, gain_ref[...])


def _mix(rnn_out, attn_out, gates, x2, ada_l, gain, w_branch, w_out, tiles_per_batch):
    t = x2.shape[0]
    tm = ROW_TILE
    row = lambda i: (i, 0)
    return pl.pallas_call(
        _mix_kernel,
        grid=(t // tm,),
        in_specs=[
            pl.BlockSpec((tm, D_MODEL), row),
            pl.BlockSpec((tm, D_MODEL), row),
            pl.BlockSpec((tm, D_GATES), row),
            pl.BlockSpec((tm, D_MODEL), row),
            _ada_spec(2, tiles_per_batch),
            _resident((1, D_MODEL), lambda i: (0, 0)),
            _resident((2, D_MODEL, D_MODEL), lambda i: (0, 0, 0)),
            _resident((D_MODEL, D_MODEL), lambda i: (0, 0)),
        ],
        out_specs=pl.BlockSpec((tm, D_MODEL), row),
        out_shape=jax.ShapeDtypeStruct((t, D_MODEL), F32),
        compiler_params=_params(("parallel",), 40),
        name="mix",
    )(rnn_out, attn_out, gates, x2, ada_l, gain, w_branch, w_out)


def _ffn_kernel(x_ref, g1_ref, sc_ref, sh_ref, gf_ref, g2_ref, wi_ref, wo_ref, o_ref, acc):
    x = x_ref[...]
    h = (_rms(x, g1_ref[...]) * (1.0 + sc_ref[...]) + sh_ref[...]).astype(BF16)
    for c in range(0, D_FF, FF_CHUNK):
        gate = jnp.dot(h, wi_ref[:, c:c + FF_CHUNK], preferred_element_type=F32)
        up = jnp.dot(h, wi_ref[:, D_FF + c:D_FF + c + FF_CHUNK], preferred_element_type=F32)
        a = (_silu(gate) * up).astype(BF16)
        part = jnp.dot(a, wo_ref[c:c + FF_CHUNK, :], preferred_element_type=F32)
        if c == 0:
            acc[...] = part
        else:
            acc[...] += part
    o_ref[...] = x + gf_ref[...] * _rms(acc[...], g2_ref[...])


def _ffn(x2, ada_l, gain1, gain2, w_in, w_out, tiles_per_batch):
    t = x2.shape[0]
    tm = ROW_TILE
    row = lambda i: (i, 0)
    return pl.pallas_call(
        _ffn_kernel,
        grid=(t // tm,),
        in_specs=[
            pl.BlockSpec((tm, D_MODEL), row),
            _resident((1, D_MODEL), lambda i: (0, 0)),
            _ada_spec(4, tiles_per_batch),
            _ada_spec(3, tiles_per_batch),
            _ada_spec(5, tiles_per_batch),
            _resident((1, D_MODEL), lambda i: (0, 0)),
            _resident((D_MODEL, 2 * D_FF), lambda i: (0, 0)),
            _resident((D_FF, D_MODEL), lambda i: (0, 0)),
        ],
        out_specs=pl.BlockSpec((tm, D_MODEL), row),
        out_shape=jax.ShapeDtypeStruct((t, D_MODEL), F32),
        scratch_shapes=[pltpu.VMEM((tm, D_MODEL), F32)],
        compiler_params=_params(("parallel",), 48),
        name="ffn",
    )(x2, gain1, ada_l, ada_l, ada_l, gain2, w_in, w_out)


def _router_kernel(x_ref, g1_ref, sc_ref, sh_ref, wr_ref, h_ref, comb_ref):
    h = _rms(x_ref[...], g1_ref[...]) * (1.0 + sc_ref[...]) + sh_ref[...]
    h_ref[...] = h.astype(h_ref.dtype)
    logits = jnp.dot(h, wr_ref[...], preferred_element_type=F32, precision=lax.Precision.HIGHEST)
    lane = lax.broadcasted_iota(jnp.int32, logits.shape, 1)
    neg = jnp.float32(-jnp.inf)
    lg = jnp.where(lane < N_EXPERTS, logits, neg)
    m1 = jnp.max(lg, axis=-1, keepdims=True)
    i1 = jnp.min(jnp.where(lg == m1, lane, LANES), axis=-1, keepdims=True)
    lg2 = jnp.where(lane == i1, neg, lg)
    m2 = jnp.max(lg2, axis=-1, keepdims=True)
    i2 = jnp.min(jnp.where(lg2 == m2, lane, LANES), axis=-1, keepdims=True)
    e2 = jnp.exp(m2 - m1)
    p1 = 1.0 / (1.0 + e2)
    p2 = e2 / (1.0 + e2)
    comb_ref[...] = jnp.where(lane == i1, p1, 0.0) + jnp.where(lane == i2, p2, 0.0)


def _router(x2, ada_l, gain1, w_router_pad, tiles_per_batch):
    t = x2.shape[0]
    tm = ROW_TILE
    row = lambda i: (i, 0)
    return pl.pallas_call(
        _router_kernel,
        grid=(t // tm,),
        in_specs=[
            pl.BlockSpec((tm, D_MODEL), row),
            _resident((1, D_MODEL), lambda i: (0, 0)),
            _ada_spec(4, tiles_per_batch),
            _ada_spec(3, tiles_per_batch),
            _resident((D_MODEL, LANES), lambda i: (0, 0)),
        ],
        out_specs=[pl.BlockSpec((tm, D_MODEL), row), pl.BlockSpec((tm, LANES), row)],
        out_shape=[jax.ShapeDtypeStruct((t, D_MODEL), BF16), jax.ShapeDtypeStruct((t, LANES), F32)],
        compiler_params=_params(("parallel",), 32),
        name="router",
    )(x2, gain1, ada_l, ada_l, w_router_pad)


def _moe_kernel(nch_ref, cnt_ref, off_ref, h_ref, srow_ref, scol_ref, comb_ref, wi_ref, wo_ref, o_ref, xg):
    s = pl.program_id(0)
    e = pl.program_id(1)
    n_sub = MOE_SUPER // MOE_SUB
    group = s * N_EXPERTS + e
    n_parts = nch_ref[group]

    @pl.when(e == 0)
    def _():
        o_ref[...] = jnp.zeros(o_ref.shape, F32)

    @pl.when(n_parts > 0)
    def _():
        last = pl.multiple_of((n_parts - 1) * MOE_PART, MOE_PART)
        xg[pl.ds(last, MOE_PART), :] = jnp.zeros((MOE_PART, D_MODEL), BF16)

    dest_col = lax.broadcasted_iota(jnp.int32, (MOE_PART, MOE_SUB), 0)
    dest_row = lax.broadcasted_iota(jnp.int32, (MOE_SUB, MOE_PART), 1)
    lane = lax.broadcasted_iota(jnp.int32, (MOE_SUB, LANES), 1)

    for j in range(n_sub):
        cnt = cnt_ref[group * n_sub + j]
        off = off_ref[group * n_sub + j]
        for part in range(MOE_SUB // MOE_PART):
            @pl.when(cnt > part * MOE_PART)
            def _():
                srow = srow_ref[:, j * MOE_SUB:(j + 1) * MOE_SUB]
                onehot = jnp.where(srow - (off + part * MOE_PART) == dest_col, 1.0, 0.0).astype(BF16)
                rows = jnp.dot(onehot, h_ref[j * MOE_SUB:(j + 1) * MOE_SUB, :], preferred_element_type=F32)
                start = pl.multiple_of(off + part * MOE_PART, MOE_ALIGN)
                xg[pl.ds(start, MOE_PART), :] = rows.astype(BF16)

    def swiglu_rows(r0, n_rows):
        x_e = xg[pl.ds(r0, n_rows), :]
        gate = jnp.dot(x_e, wi_ref[:, :D_FF_EXPERT], preferred_element_type=F32)
        up = jnp.dot(x_e, wi_ref[:, D_FF_EXPERT:], preferred_element_type=F32)
        a = (_silu(gate) * up).astype(BF16)
        xg[pl.ds(r0, n_rows), :] = jnp.dot(a, wo_ref[...], preferred_element_type=F32).astype(BF16)

    def chunk(c, carry):
        swiglu_rows(pl.multiple_of(c * MOE_CHUNK, MOE_CHUNK), MOE_CHUNK)
        return carry

    n_full = lax.shift_right_logical(n_parts, 1)
    lax.fori_loop(0, n_full, chunk, 0)

    @pl.when(n_parts % 2 == 1)
    def _():
        swiglu_rows(pl.multiple_of((n_parts - 1) * MOE_PART, MOE_PART), MOE_PART)

    for j in range(n_sub):
        cnt = cnt_ref[group * n_sub + j]
        off = off_ref[group * n_sub + j]
        for part in range(MOE_SUB // MOE_PART):
            @pl.when(cnt > part * MOE_PART)
            def _():
                rows = slice(j * MOE_SUB, (j + 1) * MOE_SUB)
                sel = lane == e
                scol = jnp.sum(jnp.where(sel, scol_ref[rows, :], 0.0), axis=-1, keepdims=True)
                prob = jnp.sum(jnp.where(sel, comb_ref[rows, :], 0.0), axis=-1, keepdims=True)
                start = pl.multiple_of(off + part * MOE_PART, MOE_ALIGN)
                onehot = jnp.where(scol.astype(jnp.int32) - start == dest_row, 1.0, 0.0).astype(BF16)
                y = xg[pl.ds(start, MOE_PART), :]
                o_ref[rows, :] += prob * jnp.dot(onehot, y, preferred_element_type=F32)


def _moe(h, comb, w_in, w_out):
    t = h.shape[0]
    n_super = t // MOE_SUPER
    n_sub = MOE_SUPER // MOE_SUB
    sel = (comb[:, :N_EXPERTS] > 0.0).reshape(n_super, n_sub, MOE_SUB, N_EXPERTS)
    seli = sel.astype(jnp.int32)
    cnt = seli.sum(axis=2)
    cnt_pad = (cnt + MOE_ALIGN - 1) // MOE_ALIGN * MOE_ALIGN
    off = jnp.cumsum(cnt_pad, axis=1) - cnt_pad
    total = cnt_pad.sum(axis=1)
    slot = jnp.where(sel, off[:, :, None, :] + jnp.cumsum(seli, axis=2) - 1, -1)
    slot = slot.reshape(n_super, MOE_SUPER, N_EXPERTS)
    slot_rows = slot.transpose(0, 2, 1).reshape(n_super, N_EXPERTS, 1, MOE_SUPER)
    slot_cols = jnp.pad(slot.reshape(t, N_EXPERTS).astype(F32), ((0, 0), (0, LANES - N_EXPERTS)),
                        constant_values=-1.0)
    n_parts = ((total + MOE_PART - 1) // MOE_PART).reshape(-1).astype(jnp.int32)
    by_group = lambda a: a.transpose(0, 2, 1).reshape(-1).astype(jnp.int32)

    xg_rows = -(-(MOE_SUPER + n_sub * (MOE_ALIGN - 1) + MOE_PART) // MOE_CHUNK) * MOE_CHUNK
    once = lambda shape, index_map: pl.BlockSpec(shape, index_map, pipeline_mode=pl.Buffered(1))
    grid_spec = pltpu.PrefetchScalarGridSpec(
        num_scalar_prefetch=3,
        grid=(n_super, N_EXPERTS),
        in_specs=[
            once((MOE_SUPER, D_MODEL), lambda s, e, *_: (s, 0)),
            pl.BlockSpec((None, None, 1, MOE_SUPER), lambda s, e, *_: (s, e, 0, 0)),
            once((MOE_SUPER, LANES), lambda s, e, *_: (s, 0)),
            once((MOE_SUPER, LANES), lambda s, e, *_: (s, 0)),
            pl.BlockSpec((None, D_MODEL, 2 * D_FF_EXPERT), lambda s, e, *_: (e, 0, 0)),
            pl.BlockSpec((None, D_FF_EXPERT, D_MODEL), lambda s, e, *_: (e, 0, 0)),
        ],
        out_specs=pl.BlockSpec((MOE_SUPER, D_MODEL), lambda s, e, *_: (s, 0)),
        scratch_shapes=[pltpu.VMEM((xg_rows, D_MODEL), BF16)],
    )
    return pl.pallas_call(
        _moe_kernel,
        grid_spec=grid_spec,
        out_shape=jax.ShapeDtypeStruct((t, D_MODEL), F32),
        compiler_params=_params(("arbitrary", "arbitrary"), 58),
        name="moe",
    )(n_parts, by_group(cnt), by_group(off), h, slot_rows, slot_cols, comb, w_in, w_out)


def _resid_kernel(x_ref, f_ref, gf_ref, g2_ref, o_ref):
    o_ref[...] = x_ref[...] + gf_ref[...] * _rms(f_ref[...], g2_ref[...])


def _resid(x2, f, ada_l, gain2, tiles_per_batch):
    t = x2.shape[0]
    tm = ROW_TILE
    row = lambda i: (i, 0)
    return pl.pallas_call(
        _resid_kernel,
        grid=(t // tm,),
        in_specs=[
            pl.BlockSpec((tm, D_MODEL), row),
            pl.BlockSpec((tm, D_MODEL), row),
            _ada_spec(5, tiles_per_batch),
            _resident((1, D_MODEL), lambda i: (0, 0)),
        ],
        out_specs=pl.BlockSpec((tm, D_MODEL), row),
        out_shape=jax.ShapeDtypeStruct((t, D_MODEL), F32),
        compiler_params=_params(("parallel",), 32),
        name="resid",
    )(x2, f, ada_l, gain2)


def kernel(x, c, w_ada, b_ada, pre_mix_gain, post_mix_gain, pre_ffn_gain, post_ffn_gain, w_in, conv_w, conv_b,
           w_rg_a, b_rg_a, w_rg_x, b_rg_x, lru_lambda, attn_sinks, w_branch, w_out, w_ffn_in, w_ffn_out,
           w_router, w_moe_in, w_moe_out):
    batch, seq, d = x.shape
    depth = w_in.shape[0]
    t = batch * seq
    tiles_per_batch = seq // ROW_TILE
    row1 = lambda v: v.reshape(1, -1)

    ada = _ada(c, w_ada, b_ada)
    x2 = x.reshape(t, d)
    for l in range(depth):
        ada_l = ada[l]
        xr, gr, q, kv, gates = _inproj(x2, row1(pre_mix_gain[l]), ada_l, w_in[l].astype(BF16), tiles_per_batch)
        w_gate = jnp.concatenate([w_rg_a[l], w_rg_x[l]], axis=-1).astype(BF16)
        rnn_out = _rnn(xr, gr, conv_w[l], row1(conv_b[l]), w_gate, row1(b_rg_a[l]), row1(b_rg_x[l]),
                       row1(lru_lambda[l]), batch, seq)
        attn_out = _attn(q, kv, attn_sinks[l], batch, seq)
        x2 = _mix(rnn_out, attn_out, gates, x2, ada_l, row1(post_mix_gain[l]), w_branch[l].astype(BF16),
                  w_out[l].astype(BF16), tiles_per_batch)
        if l % 2 == 0:
            x2 = _ffn(x2, ada_l, row1(pre_ffn_gain[l]), row1(post_ffn_gain[l]), w_ffn_in[l // 2].astype(BF16),
                      w_ffn_out[l // 2].astype(BF16), tiles_per_batch)
        else:
            w_r = jnp.pad(w_router[l // 2], ((0, 0), (0, LANES - N_EXPERTS)))
            h, comb = _router(x2, ada_l, row1(pre_ffn_gain[l]), w_r, tiles_per_batch)
            f = _moe(h, comb, w_moe_in[l // 2].astype(BF16), w_moe_out[l // 2].astype(BF16))
            x2 = _resid(x2, f, ada_l, row1(post_ffn_gain[l]), tiles_per_batch)
    return x2.reshape(batch, seq, d)
```

```python
import functools
import math

import jax
import jax.numpy as jnp
from jax import lax
from jax.experimental import pallas as pl
from jax.experimental.pallas import tpu as pltpu

D_MODEL = 1024
D_RNN = 1024
N_RNN_BLOCKS = 8
RNN_BLOCK = D_RNN // N_RNN_BLOCKS
CONV_WIDTH = 4
LRU_C = 8.0
N_HEADS = 8
N_KV_HEADS = 2
GROUP = N_HEADS // N_KV_HEADS
HEAD_DIM = 128
WINDOW = 128
D_ATTN = N_HEADS * HEAD_DIM
D_KV = N_KV_HEADS * HEAD_DIM
D_GATES = 2 * D_MODEL
D_IN = 2 * D_RNN + D_ATTN + 2 * D_KV + D_GATES
D_FF = 3 * D_MODEL
N_EXPERTS = 8
D_FF_EXPERT = D_FF // 2
EPS = 1e-6

BF16 = jnp.bfloat16
F32 = jnp.float32

SUBLANES = 8
LANES = 128
MIB = 1024 * 1024

ROW_TILE = 512
SEQ_TILE = 512
Q_TILE = 512
PROJ_CHUNK = 512
FF_CHUNK = 512
MOE_SUPER = 2048
MOE_SUB = 256
MOE_PART = 128
MOE_ALIGN = 16
MOE_CHUNK = 256
MASK_VALUE = -1e30
LOG2E = math.log2(math.e)


def _params(semantics, vmem_mib):
    return pltpu.CompilerParams(dimension_semantics=semantics, vmem_limit_bytes=vmem_mib * MIB)


def _resident(shape, index_map):
    return pl.BlockSpec(shape, index_map, pipeline_mode=pl.Buffered(1))


def _rms(x, gain):
    return x * lax.rsqrt(jnp.mean(x * x, axis=-1, keepdims=True) + EPS) * gain


def _gelu_tanh(x):
    c = math.sqrt(2.0 / math.pi)
    return 0.5 * x * (1.0 + jnp.tanh(c * (x + 0.044715 * (x * x * x))))


def _silu(x):
    return x * jax.nn.sigmoid(x)


def _ada_kernel(c_ref, w_ref, b_ref, o_ref):
    c = c_ref[...]
    o_ref[...] = jnp.dot(_silu(c), w_ref[...], preferred_element_type=F32,
                         precision=lax.Precision.HIGHEST) + b_ref[...]


def _ada(c, w_ada, b_ada):
    n_layers, d, n6 = w_ada.shape
    b = c.shape[0]
    c_pad = jnp.zeros((SUBLANES, d), F32).at[:b].set(c)
    nb = 1536
    out = pl.pallas_call(
        _ada_kernel,
        grid=(n_layers, n6 // nb),
        in_specs=[
            pl.BlockSpec((SUBLANES, d), lambda l, j: (0, 0)),
            pl.BlockSpec((None, d, nb), lambda l, j: (l, 0, j)),
            pl.BlockSpec((None, 1, nb), lambda l, j: (l, 0, j)),
        ],
        out_specs=pl.BlockSpec((None, SUBLANES, nb), lambda l, j: (l, 0, j)),
        out_shape=jax.ShapeDtypeStruct((n_layers, SUBLANES, n6), F32),
        compiler_params=_params(("arbitrary", "arbitrary"), 32),
        name="ada",
    )(c_pad, w_ada, b_ada.reshape(n_layers, 1, n6))
    return out[:, :b].reshape(n_layers, b, 1, n6)


def _ada_spec(col, tiles_per_batch, lead=0):
    return pl.BlockSpec((None, 1, D_MODEL),
                        lambda i, *_: (jnp.maximum(i - lead, 0) // tiles_per_batch, 0, col))


W_STEPS = 8


def _row_spec(tm, width):
    return pl.BlockSpec((tm, width), lambda i: (jnp.maximum(i - W_STEPS, 0), 0))


def _weight_chunk_spec(rows, cols):
    return pl.BlockSpec((rows // W_STEPS, cols), lambda i: (jnp.minimum(i, W_STEPS - 1), 0))


def _keep_weight_chunk(i, w_ref, w_scr):
    rows = w_ref.shape[0]
    w_scr[pl.ds(pl.multiple_of(i * rows, rows), rows), :] = w_ref[...].astype(BF16)


def _inproj_kernel(x_ref, gain_ref, sc_ref, sh_ref, w_ref, xr_ref, gr_ref, q_ref, kv_ref, gt_ref, w_scr):
    i = pl.program_id(0)

    @pl.when(i < W_STEPS)
    def _():
        _keep_weight_chunk(i, w_ref, w_scr)

    @pl.when(i >= W_STEPS)
    def _():
        h = _rms(x_ref[...], gain_ref[...]) * (1.0 + sc_ref[...]) + sh_ref[...]
        h = h.astype(BF16)
        outs = ((xr_ref, D_RNN), (gr_ref, D_RNN), (q_ref, D_ATTN), (kv_ref, 2 * D_KV), (gt_ref, D_GATES))
        col = 0
        for ref, width in outs:
            for c in range(0, width, PROJ_CHUNK):
                ref[:, c:c + PROJ_CHUNK] = jnp.dot(
                    h, w_scr[:, col + c:col + c + PROJ_CHUNK], preferred_element_type=F32).astype(ref.dtype)
            col += width


def _inproj(x2, gain, ada_l, w_in, tiles_per_batch):
    t = x2.shape[0]
    tm = ROW_TILE
    widths = (D_RNN, D_RNN, D_ATTN, 2 * D_KV, D_GATES)
    return pl.pallas_call(
        _inproj_kernel,
        grid=(W_STEPS + t // tm,),
        in_specs=[
            _row_spec(tm, D_MODEL),
            _resident((1, D_MODEL), lambda i: (0, 0)),
            _ada_spec(1, tiles_per_batch, W_STEPS),
            _ada_spec(0, tiles_per_batch, W_STEPS),
            _weight_chunk_spec(D_MODEL, D_IN),
        ],
        out_specs=[_row_spec(tm, w) for w in widths],
        out_shape=[jax.ShapeDtypeStruct((t, w), BF16) for w in widths],
        scratch_shapes=[pltpu.VMEM((D_MODEL, D_IN), BF16)],
        compiler_params=_params(("arbitrary",), 48),
        name="inproj",
    )(x2, gain, ada_l, ada_l, w_in)


def _rnn_kernel(xr_ref, gr_ref, perm_ref, unperm_ref, cw_ref, cb_ref, wg_ref, ba_ref, bx_ref, lam_ref, o_ref,
                xs, gs, outp, hist, hcar):
    ts = xr_ref.shape[0]
    seg = ts // SUBLANES
    t = pl.program_id(1)

    @pl.when(t == 0)
    def _():
        hist[...] = jnp.zeros((SUBLANES, D_RNN), F32)
        hcar[...] = jnp.zeros((SUBLANES, D_RNN), F32)

    sub = lax.broadcasted_iota(jnp.int32, (SUBLANES, RNN_BLOCK), 0)
    xs[...] = jnp.dot(perm_ref[...], xr_ref[...], preferred_element_type=F32)
    gs[...] = jnp.dot(perm_ref[...], gr_ref[...], preferred_element_type=F32)
    lam = lam_ref[...]
    sp2 = (-LRU_C * LOG2E) * (jnp.maximum(-lam, 0.0) + jnp.log1p(jnp.exp(-jnp.abs(lam))))
    c = math.sqrt(2.0 / math.pi)
    vrow = lambda arr, v: arr[v * SUBLANES:(v + 1) * SUBLANES, :]

    for n in range(N_RNN_BLOCKS):
        cols = slice(n * RNN_BLOCK, (n + 1) * RNN_BLOCK)
        xp = [xs[v * SUBLANES:(v + 1) * SUBLANES, cols] for v in range(seg)]

        def tail(j):
            rolled = pltpu.roll(xp[seg - j], 1, axis=0)
            return jnp.where(sub == 0, jnp.broadcast_to(hist[j:j + 1, cols], (SUBLANES, RNN_BLOCK)), rolled)

        tails = {j: tail(j) for j in range(1, CONV_WIDTH)}
        for j in range(1, CONV_WIDTH):
            hist[j:j + 1, cols] = xp[seg - j][SUBLANES - 1:SUBLANES, :]

        xc = cb_ref[:, cols] + cw_ref[CONV_WIDTH - 1:CONV_WIDTH, cols] * jnp.concatenate(xp, axis=0)
        for j in range(1, CONV_WIDTH):
            k = CONV_WIDTH - 1 - j
            shifted = jnp.concatenate([tails[j - v] for v in range(j)] + xp[:seg - j], axis=0)
            xc = xc + cw_ref[k:k + 1, cols] * shifted

        g = jnp.dot(xc.astype(BF16), wg_ref[n], preferred_element_type=F32)
        r = jax.nn.sigmoid(g[:, :RNN_BLOCK] + ba_ref[:, cols])
        i = jax.nn.sigmoid(g[:, RNN_BLOCK:] + bx_ref[:, cols])
        a = jnp.exp2(r * sp2[:, cols])
        w = 1.0 - a * a
        u = (w * lax.rsqrt(jnp.maximum(w, 1e-30))) * i * xc

        h = jnp.zeros((SUBLANES, RNN_BLOCK), F32)
        prod = jnp.ones((SUBLANES, RNN_BLOCK), F32)
        hs, prods = [], []
        for v in range(seg):
            h = vrow(a, v) * h + vrow(u, v)
            prod = vrow(a, v) * prod
            hs.append(h)
            prods.append(prod)

        pa, pb = prod, h
        for s in (1, 2, 4):
            a_sh = pltpu.roll(pa, s, axis=0)
            b_sh = pltpu.roll(pb, s, axis=0)
            m = sub >= s
            pb = jnp.where(m, pa * b_sh + pb, pb)
            pa = jnp.where(m, pa * a_sh, pa)
        h_end = pb + pa * hcar[:, cols]
        carry_in = jnp.where(sub == 0, hcar[:, cols], pltpu.roll(h_end, 1, axis=0))
        hcar[:, cols] = jnp.broadcast_to(h_end[SUBLANES - 1:SUBLANES, :], (SUBLANES, RNN_BLOCK))

        h_all = jnp.concatenate([hs[v] + prods[v] * carry_in for v in range(seg)], axis=0)

        gg = gs[:, cols]
        th = jnp.tanh(gg * (c + (c * 0.044715) * (gg * gg)))
        hg = h_all * (0.5 * gg)
        outp[:, cols] = (hg + hg * th).astype(BF16)

    o_ref[...] = jnp.dot(unperm_ref[...], outp[...], preferred_element_type=F32).astype(o_ref.dtype)


def _segment_permutation(ts):
    seg = ts // SUBLANES
    p = jnp.arange(ts)
    src = (p % SUBLANES) * seg + p // SUBLANES
    perm = (src[:, None] == jnp.arange(ts)[None, :]).astype(BF16)
    return perm, perm.T


def _rnn(xr, gr, conv_w, conv_b, w_gate, b_a, b_x, lam, batch, seq):
    ts = SEQ_TILE
    nt = seq // ts
    row = lambda b, t: (b * nt + t, 0)
    const2 = lambda b, t: (0, 0)
    perm, unperm = _segment_permutation(ts)
    return pl.pallas_call(
        _rnn_kernel,
        grid=(batch, nt),
        in_specs=[
            pl.BlockSpec((ts, D_RNN), row),
            pl.BlockSpec((ts, D_RNN), row),
            _resident((ts, ts), const2),
            _resident((ts, ts), const2),
            _resident((CONV_WIDTH, D_RNN), const2),
            _resident((1, D_RNN), const2),
            _resident((N_RNN_BLOCKS, RNN_BLOCK, 2 * RNN_BLOCK), lambda b, t: (0, 0, 0)),
            _resident((1, D_RNN), const2),
            _resident((1, D_RNN), const2),
            _resident((1, D_RNN), const2),
        ],
        out_specs=pl.BlockSpec((ts, D_RNN), row),
        out_shape=jax.ShapeDtypeStruct((batch * seq, D_RNN), BF16),
        scratch_shapes=[
            pltpu.VMEM((ts, D_RNN), F32),
            pltpu.VMEM((ts, D_RNN), F32),
            pltpu.VMEM((ts, D_RNN), BF16),
            pltpu.VMEM((SUBLANES, D_RNN), F32),
            pltpu.VMEM((SUBLANES, D_RNN), F32),
        ],
        compiler_params=_params(("arbitrary", "arbitrary"), 40),
        name="rnn",
    )(xr, gr, perm, unperm, conv_w, conv_b, w_gate, b_a, b_x, lam)


def _attn_bias(sinks):
    qi = jnp.arange(WINDOW)[:, None]
    sj = jnp.arange(2 * WINDOW)[None, :]
    dist = (qi + WINDOW - sj).astype(F32)
    valid = (dist >= 0) & (dist < WINDOW)
    slopes = jnp.asarray([2.0 ** (-8.0 * (h + 1) / N_HEADS) for h in range(N_HEADS)], F32)
    bias = jnp.where(valid[None], -slopes[:, None, None] * dist[None], MASK_VALUE)
    bias = jnp.where((sj == 0)[None], sinks.astype(F32)[:, None, None], bias)
    return (bias * LOG2E).reshape(N_KV_HEADS, GROUP * WINDOW, 2 * WINDOW)


def _attn_kernel(q_ref, kv_ref, kvp_ref, bias_ref, o_ref, s_scr, p_scr):
    tq = q_ref.shape[0]
    first = pl.program_id(1) == 0
    col = lax.broadcasted_iota(jnp.int32, (GROUP * WINDOW, 2 * WINDOW), 1)
    first_mask = jnp.where(first & (col >= 1) & (col < WINDOW), MASK_VALUE, 0.0).astype(F32)
    slot0 = lax.broadcasted_iota(jnp.int32, (2 * WINDOW, HEAD_DIM), 0) == 0
    zeros = jnp.zeros((2 * WINDOW, HEAD_DIM), BF16)
    ones = jnp.ones((2 * WINDOW, HEAD_DIM), BF16)
    scale2 = (HEAD_DIM ** -0.5) * LOG2E
    units = [(jb, kvh) for jb in range(tq // WINDOW) for kvh in range(N_KV_HEADS)]

    def keys_or_values(jb, cols):
        r0 = jb * WINDOW
        prev = kvp_ref[:, cols] if jb == 0 else kv_ref[r0 - WINDOW:r0, cols]
        kv = jnp.concatenate([prev, kv_ref[r0:r0 + WINDOW, cols]], axis=0)
        return jnp.where(slot0, zeros, kv)

    for u, (jb, kvh) in enumerate(units):
        r0 = jb * WINDOW
        k = keys_or_values(jb, slice(kvh * HEAD_DIM, (kvh + 1) * HEAD_DIM))
        q = jnp.concatenate(
            [q_ref[r0:r0 + WINDOW, (kvh * GROUP + g) * HEAD_DIM:(kvh * GROUP + g + 1) * HEAD_DIM]
             for g in range(GROUP)], axis=0)
        s = lax.dot_general(q, k, (((1,), (1,)), ((), ())), preferred_element_type=F32)
        s = s * scale2 + bias_ref[kvh]
        if jb == 0:
            s = s + first_mask
        s_scr[u] = s

    for u, (jb, kvh) in enumerate(units):
        r0 = jb * WINDOW
        v = keys_or_values(jb, slice(D_KV + kvh * HEAD_DIM, D_KV + (kvh + 1) * HEAD_DIM))
        v_ext = jnp.concatenate([v, ones], axis=1)
        for g in range(GROUP):
            rows = slice(g * WINDOW, (g + 1) * WINDOW)
            s = s_scr[u, rows, :]
            m = jnp.max(s, axis=-1, keepdims=True)
            p_scr[u, rows, :] = jnp.exp2(s - m).astype(BF16)
        o_ext = jnp.dot(p_scr[u], v_ext, preferred_element_type=F32)
        o = o_ext[:, :HEAD_DIM] / o_ext[:, HEAD_DIM:]
        for g in range(GROUP):
            h = kvh * GROUP + g
            o_ref[r0:r0 + WINDOW, h * HEAD_DIM:(h + 1) * HEAD_DIM] = (
                o[g * WINDOW:(g + 1) * WINDOW, :].astype(o_ref.dtype))


def _attn(q, kv, sinks, batch, seq):
    tq = Q_TILE
    nq = seq // tq
    blocks_per_tile = tq // WINDOW
    blocks_per_seq = seq // WINDOW
    row = lambda b, i: (b * nq + i, 0)
    prev = lambda b, i: (b * blocks_per_seq + jnp.maximum(i * blocks_per_tile - 1, 0), 0)
    return pl.pallas_call(
        _attn_kernel,
        grid=(batch, nq),
        in_specs=[
            pl.BlockSpec((tq, D_ATTN), row),
            pl.BlockSpec((tq, 2 * D_KV), row),
            pl.BlockSpec((WINDOW, 2 * D_KV), prev),
            _resident((N_KV_HEADS, GROUP * WINDOW, 2 * WINDOW), lambda b, i: (0, 0, 0)),
        ],
        out_specs=pl.BlockSpec((tq, D_ATTN), row),
        out_shape=jax.ShapeDtypeStruct((batch * seq, D_ATTN), BF16),
        scratch_shapes=[pltpu.VMEM((blocks_per_tile * N_KV_HEADS, GROUP * WINDOW, 2 * WINDOW), F32),
                        pltpu.VMEM((blocks_per_tile * N_KV_HEADS, GROUP * WINDOW, 2 * WINDOW), BF16)],
        compiler_params=_params(("parallel", "parallel"), 32),
        name="attn",
    )(q, kv, kv, _attn_bias(sinks))


def _mix_kernel(rnn_ref, att_ref, gt_ref, x_ref, gm_ref, gain_ref, wb_ref, wo_ref, o_ref, wb_scr, wo_scr):
    i = pl.program_id(0)

    @pl.when(i < W_STEPS)
    def _():
        _keep_weight_chunk(i, wb_ref, wb_scr)
        _keep_weight_chunk(i, wo_ref, wo_scr)

    @pl.when(i >= W_STEPS)
    def _():
        bp0 = jnp.dot(rnn_ref[...], wb_scr[:D_MODEL, :], preferred_element_type=F32)
        bp1 = jnp.dot(att_ref[...], wb_scr[D_MODEL:, :], preferred_element_type=F32)
        g0 = jax.nn.sigmoid(gt_ref[:, :D_MODEL].astype(F32))
        g1 = jax.nn.sigmoid(gt_ref[:, D_MODEL:].astype(F32))
        merged = (g0 * bp0 + g1 * bp1).astype(BF16)
        mix = jnp.dot(merged, wo_scr[...], preferred_element_type=F32)
        o_ref[...] = x_ref[...] + gm_ref[...] * _rms(mix, gain_ref[...])


def _mix(rnn_out, attn_out, gates, x2, ada_l, gain, w_branch, w_out, tiles_per_batch):
    t = x2.shape[0]
    tm = ROW_TILE
    return pl.pallas_call(
        _mix_kernel,
        grid=(W_STEPS + t // tm,),
        in_specs=[
            _row_spec(tm, D_MODEL),
            _row_spec(tm, D_MODEL),
            _row_spec(tm, D_GATES),
            _row_spec(tm, D_MODEL),
            _ada_spec(2, tiles_per_batch, W_STEPS),
            _resident((1, D_MODEL), lambda i: (0, 0)),
            _weight_chunk_spec(2 * D_MODEL, D_MODEL),
            _weight_chunk_spec(D_MODEL, D_MODEL),
        ],
        out_specs=_row_spec(tm, D_MODEL),
        out_shape=jax.ShapeDtypeStruct((t, D_MODEL), F32),
        scratch_shapes=[pltpu.VMEM((2 * D_MODEL, D_MODEL), BF16), pltpu.VMEM((D_MODEL, D_MODEL), BF16)],
        compiler_params=_params(("arbitrary",), 40),
        name="mix",
    )(rnn_out, attn_out, gates, x2, ada_l, gain, w_branch.reshape(2 * D_MODEL, D_MODEL), w_out)


def _ffn_kernel(x_ref, g1_ref, sc_ref, sh_ref, gf_ref, g2_ref, wi_ref, wo_ref, o_ref, wi_scr, wo_scr, acc):
    i = pl.program_id(0)

    @pl.when(i < W_STEPS)
    def _():
        _keep_weight_chunk(i, wi_ref, wi_scr)
        _keep_weight_chunk(i, wo_ref, wo_scr)

    @pl.when(i >= W_STEPS)
    def _():
        x = x_ref[...]
        h = (_rms(x, g1_ref[...]) * (1.0 + sc_ref[...]) + sh_ref[...]).astype(BF16)
        for c in range(0, D_FF, FF_CHUNK):
            gate = jnp.dot(h, wi_scr[:, c:c + FF_CHUNK], preferred_element_type=F32)
            up = jnp.dot(h, wi_scr[:, D_FF + c:D_FF + c + FF_CHUNK], preferred_element_type=F32)
            a = (_silu(gate) * up).astype(BF16)
            part = jnp.dot(a, wo_scr[c:c + FF_CHUNK, :], preferred_element_type=F32)
            if c == 0:
                acc[...] = part
            else:
                acc[...] += part
        o_ref[...] = x + gf_ref[...] * _rms(acc[...], g2_ref[...])


def _ffn(x2, ada_l, gain1, gain2, w_in, w_out, tiles_per_batch):
    t = x2.shape[0]
    tm = ROW_TILE
    return pl.pallas_call(
        _ffn_kernel,
        grid=(W_STEPS + t // tm,),
        in_specs=[
            _row_spec(tm, D_MODEL),
            _resident((1, D_MODEL), lambda i: (0, 0)),
            _ada_spec(4, tiles_per_batch, W_STEPS),
            _ada_spec(3, tiles_per_batch, W_STEPS),
            _ada_spec(5, tiles_per_batch, W_STEPS),
            _resident((1, D_MODEL), lambda i: (0, 0)),
            _weight_chunk_spec(D_MODEL, 2 * D_FF),
            _weight_chunk_spec(D_FF, D_MODEL),
        ],
        out_specs=_row_spec(tm, D_MODEL),
        out_shape=jax.ShapeDtypeStruct((t, D_MODEL), F32),
        scratch_shapes=[pltpu.VMEM((D_MODEL, 2 * D_FF), BF16), pltpu.VMEM((D_FF, D_MODEL), BF16),
                        pltpu.VMEM((tm, D_MODEL), F32)],
        compiler_params=_params(("arbitrary",), 56),
        name="ffn",
    )(x2, gain1, ada_l, ada_l, ada_l, gain2, w_in, w_out)


def _router_kernel(x_ref, g1_ref, sc_ref, sh_ref, wr_ref, h_ref, comb_ref, rank_ref):
    h = _rms(x_ref[...], g1_ref[...]) * (1.0 + sc_ref[...]) + sh_ref[...]
    h_ref[...] = h.astype(h_ref.dtype)
    logits = jnp.dot(h, wr_ref[...], preferred_element_type=F32, precision=lax.Precision.HIGHEST)
    lane = lax.broadcasted_iota(jnp.int32, logits.shape, 1)
    neg = jnp.float32(-jnp.inf)
    lg = jnp.where(lane < N_EXPERTS, logits, neg)
    m1 = jnp.max(lg, axis=-1, keepdims=True)
    i1 = jnp.min(jnp.where(lg == m1, lane, LANES), axis=-1, keepdims=True)
    lg2 = jnp.where(lane == i1, neg, lg)
    m2 = jnp.max(lg2, axis=-1, keepdims=True)
    i2 = jnp.min(jnp.where(lg2 == m2, lane, LANES), axis=-1, keepdims=True)
    e2 = jnp.exp(m2 - m1)
    p1 = 1.0 / (1.0 + e2)
    p2 = e2 / (1.0 + e2)
    comb = jnp.where(lane == i1, p1, 0.0) + jnp.where(lane == i2, p2, 0.0)
    comb_ref[...] = comb
    sel = jnp.where(comb > 0.0, 1.0, 0.0).astype(BF16)
    r = lax.broadcasted_iota(jnp.int32, (MOE_SUB, MOE_SUB), 0)
    c = lax.broadcasted_iota(jnp.int32, (MOE_SUB, MOE_SUB), 1)
    tri = jnp.where(c <= r, 1.0, 0.0).astype(BF16)
    for r0 in range(0, comb.shape[0], MOE_SUB):
        upto = jnp.dot(tri, sel[r0:r0 + MOE_SUB, :], preferred_element_type=F32)
        rank_ref[r0:r0 + MOE_SUB, :] = jnp.where(comb[r0:r0 + MOE_SUB, :] > 0.0, upto - 1.0, -1.0)


def _router(x2, ada_l, gain1, w_router_pad, tiles_per_batch):
    t = x2.shape[0]
    tm = ROW_TILE
    row = lambda i: (i, 0)
    return pl.pallas_call(
        _router_kernel,
        grid=(t // tm,),
        in_specs=[
            pl.BlockSpec((tm, D_MODEL), row),
            _resident((1, D_MODEL), lambda i: (0, 0)),
            _ada_spec(4, tiles_per_batch),
            _ada_spec(3, tiles_per_batch),
            _resident((D_MODEL, LANES), lambda i: (0, 0)),
        ],
        out_specs=[pl.BlockSpec((tm, D_MODEL), row), pl.BlockSpec((tm, LANES), row),
                   pl.BlockSpec((tm, LANES), row)],
        out_shape=[jax.ShapeDtypeStruct((t, D_MODEL), BF16), jax.ShapeDtypeStruct((t, LANES), F32),
                   jax.ShapeDtypeStruct((t, LANES), F32)],
        compiler_params=_params(("parallel",), 32),
        name="router",
    )(x2, gain1, ada_l, ada_l, w_router_pad)


def _moe_kernel(nch_ref, cnt_ref, off_ref, h_ref, srow_ref, scol_ref, comb_ref, wi_ref, wo_ref, o_ref, xg):
    s = pl.program_id(0)
    e = pl.program_id(1)
    n_sub = MOE_SUPER // MOE_SUB
    group = s * N_EXPERTS + e
    n_parts = nch_ref[group]

    @pl.when(e == 0)
    def _():
        o_ref[...] = jnp.zeros(o_ref.shape, F32)

    @pl.when(n_parts > 0)
    def _():
        last = pl.multiple_of((n_parts - 1) * MOE_PART, MOE_PART)
        xg[pl.ds(last, MOE_PART), :] = jnp.zeros((MOE_PART, D_MODEL), BF16)

    dest_col = lax.broadcasted_iota(jnp.int32, (MOE_PART, MOE_SUB), 0).astype(F32)
    dest_row = lax.broadcasted_iota(jnp.int32, (MOE_SUB, MOE_PART), 1).astype(F32)
    lane = lax.broadcasted_iota(jnp.int32, (MOE_SUB, LANES), 1)

    def gathered(j, part):
        srow = srow_ref[:, j * MOE_SUB:(j + 1) * MOE_SUB]
        onehot = jnp.where(srow - float(part * MOE_PART) == dest_col, 1.0, 0.0).astype(BF16)
        rows = jnp.dot(onehot, h_ref[j * MOE_SUB:(j + 1) * MOE_SUB, :], preferred_element_type=F32)
        return rows.astype(BF16)

    for j in range(n_sub):
        start = pl.multiple_of(off_ref[group * n_sub + j], MOE_ALIGN)
        xg[pl.ds(start, MOE_PART), :] = gathered(j, 0)

    part_row = lax.broadcasted_iota(jnp.int32, (MOE_PART, D_MODEL), 0)
    for j in range(n_sub):
        cnt = cnt_ref[group * n_sub + j]
        for part in range(1, MOE_SUB // MOE_PART):
            @pl.when(cnt > part * MOE_PART)
            def _():
                start = pl.multiple_of(off_ref[group * n_sub + j] + part * MOE_PART, MOE_ALIGN)
                cnt_pad = jnp.bitwise_and(cnt + (MOE_ALIGN - 1), -MOE_ALIGN)
                own = part_row < cnt_pad - part * MOE_PART
                xg[pl.ds(start, MOE_PART), :] = jnp.where(own, gathered(j, part), xg[pl.ds(start, MOE_PART), :])

    def swiglu_rows(r0, n_rows):
        x_e = xg[pl.ds(r0, n_rows), :]
        gate = jnp.dot(x_e, wi_ref[:, :D_FF_EXPERT], preferred_element_type=F32)
        up = jnp.dot(x_e, wi_ref[:, D_FF_EXPERT:], preferred_element_type=F32)
        a = (_silu(gate) * up).astype(BF16)
        xg[pl.ds(r0, n_rows), :] = jnp.dot(a, wo_ref[...], preferred_element_type=F32).astype(BF16)

    def chunk(c, carry):
        swiglu_rows(pl.multiple_of(c * MOE_CHUNK, MOE_CHUNK), MOE_CHUNK)
        return carry

    n_full = lax.shift_right_logical(n_parts, 1)
    lax.fori_loop(0, n_full, chunk, 0)

    @pl.when(n_parts % 2 == 1)
    def _():
        swiglu_rows(pl.multiple_of((n_parts - 1) * MOE_PART, MOE_PART), MOE_PART)

    def scatter(j, part):
        rows = slice(j * MOE_SUB, (j + 1) * MOE_SUB)
        sel = lane == e
        scol = jnp.sum(jnp.where(sel, scol_ref[rows, :], 0.0), axis=-1, keepdims=True)
        prob = jnp.sum(jnp.where(sel, comb_ref[rows, :], 0.0), axis=-1, keepdims=True)
        start = pl.multiple_of(off_ref[group * n_sub + j] + part * MOE_PART, MOE_ALIGN)
        onehot = jnp.where(scol - float(part * MOE_PART) == dest_row, 1.0, 0.0).astype(BF16)
        y = xg[pl.ds(start, MOE_PART), :]
        if part > 0:
            cnt_pad = jnp.bitwise_and(cnt_ref[group * n_sub + j] + (MOE_ALIGN - 1), -MOE_ALIGN)
            y = jnp.where(part_row < cnt_pad - part * MOE_PART, y, jnp.zeros_like(y))
        o_ref[rows, :] += prob * jnp.dot(onehot, y, preferred_element_type=F32)

    for j in range(n_sub):
        scatter(j, 0)
    for j in range(n_sub):
        cnt = cnt_ref[group * n_sub + j]
        for part in range(1, MOE_SUB // MOE_PART):
            pl.when(cnt > part * MOE_PART)(functools.partial(scatter, j, part))


def _moe(h, comb, rank, w_in, w_out):
    t = h.shape[0]
    n_super = t // MOE_SUPER
    n_sub = MOE_SUPER // MOE_SUB
    sel = (comb[:, :N_EXPERTS] > 0.0).reshape(n_super, n_sub, MOE_SUB, N_EXPERTS)
    cnt = sel.astype(jnp.int32).sum(axis=2)
    cnt_pad = (cnt + MOE_ALIGN - 1) // MOE_ALIGN * MOE_ALIGN
    off = jnp.cumsum(cnt_pad, axis=1) - cnt_pad
    total = cnt_pad.sum(axis=1)
    rank_rows = (rank[:, :N_EXPERTS].reshape(n_super, MOE_SUPER, N_EXPERTS).transpose(0, 2, 1)
                 .reshape(n_super, N_EXPERTS, 1, MOE_SUPER))
    n_parts = ((total + MOE_PART - 1) // MOE_PART).reshape(-1).astype(jnp.int32)
    by_group = lambda a: a.transpose(0, 2, 1).reshape(-1).astype(jnp.int32)

    xg_rows = -(-(MOE_SUPER + n_sub * (MOE_ALIGN - 1) + MOE_PART) // MOE_CHUNK) * MOE_CHUNK
    once = lambda shape, index_map: pl.BlockSpec(shape, index_map, pipeline_mode=pl.Buffered(1))
    grid_spec = pltpu.PrefetchScalarGridSpec(
        num_scalar_prefetch=3,
        grid=(n_super, N_EXPERTS),
        in_specs=[
            once((MOE_SUPER, D_MODEL), lambda s, e, *_: (s, 0)),
            pl.BlockSpec((None, None, 1, MOE_SUPER), lambda s, e, *_: (s, e, 0, 0)),
            once((MOE_SUPER, LANES), lambda s, e, *_: (s, 0)),
            once((MOE_SUPER, LANES), lambda s, e, *_: (s, 0)),
            pl.BlockSpec((None, D_MODEL, 2 * D_FF_EXPERT), lambda s, e, *_: (e, 0, 0)),
            pl.BlockSpec((None, D_FF_EXPERT, D_MODEL), lambda s, e, *_: (e, 0, 0)),
        ],
        out_specs=pl.BlockSpec((MOE_SUPER, D_MODEL), lambda s, e, *_: (s, 0)),
        scratch_shapes=[pltpu.VMEM((xg_rows, D_MODEL), BF16)],
    )
    return pl.pallas_call(
        _moe_kernel,
        grid_spec=grid_spec,
        out_shape=jax.ShapeDtypeStruct((t, D_MODEL), F32),
        compiler_params=_params(("arbitrary", "arbitrary"), 58),
        name="moe",
    )(n_parts, by_group(cnt), by_group(off), h, rank_rows, rank, comb, w_in, w_out)


def _resid_kernel(x_ref, f_ref, gf_ref, g2_ref, o_ref):
    o_ref[...] = x_ref[...] + gf_ref[...] * _rms(f_ref[...], g2_ref[...])


def _resid(x2, f, ada_l, gain2, tiles_per_batch):
    t = x2.shape[0]
    tm = ROW_TILE
    row = lambda i: (i, 0)
    return pl.pallas_call(
        _resid_kernel,
        grid=(t // tm,),
        in_specs=[
            pl.BlockSpec((tm, D_MODEL), row),
            pl.BlockSpec((tm, D_MODEL), row),
            _ada_spec(5, tiles_per_batch),
            _resident((1, D_MODEL), lambda i: (0, 0)),
        ],
        out_specs=pl.BlockSpec((tm, D_MODEL), row),
        out_shape=jax.ShapeDtypeStruct((t, D_MODEL), F32),
        compiler_params=_params(("parallel",), 32),
        name="resid",
    )(x2, f, ada_l, gain2)


def kernel(x, c, w_ada, b_ada, pre_mix_gain, post_mix_gain, pre_ffn_gain, post_ffn_gain, w_in, conv_w, conv_b,
           w_rg_a, b_rg_a, w_rg_x, b_rg_x, lru_lambda, attn_sinks, w_branch, w_out, w_ffn_in, w_ffn_out,
           w_router, w_moe_in, w_moe_out):
    batch, seq, d = x.shape
    depth = w_in.shape[0]
    t = batch * seq
    tiles_per_batch = seq // ROW_TILE
    row1 = lambda v: v.reshape(1, -1)

    ada = _ada(c, w_ada, b_ada)
    x2 = x.reshape(t, d)
    for l in range(depth):
        ada_l = ada[l]
        xr, gr, q, kv, gates = _inproj(x2, row1(pre_mix_gain[l]), ada_l, w_in[l], tiles_per_batch)
        w_gate = jnp.concatenate([w_rg_a[l], w_rg_x[l]], axis=-1).astype(BF16)
        rnn_out = _rnn(xr, gr, conv_w[l], row1(conv_b[l]), w_gate, row1(b_rg_a[l]), row1(b_rg_x[l]),
                       row1(lru_lambda[l]), batch, seq)
        attn_out = _attn(q, kv, attn_sinks[l], batch, seq)
        x2 = _mix(rnn_out, attn_out, gates, x2, ada_l, row1(post_mix_gain[l]), w_branch[l], w_out[l],
                  tiles_per_batch)
        if l % 2 == 0:
            x2 = _ffn(x2, ada_l, row1(pre_ffn_gain[l]), row1(post_ffn_gain[l]), w_ffn_in[l // 2],
                      w_ffn_out[l // 2], tiles_per_batch)
        else:
            w_r = jnp.pad(w_router[l // 2], ((0, 0), (0, LANES - N_EXPERTS)))
            h, comb, rank = _router(x2, ada_l, row1(pre_ffn_gain[l]), w_r, tiles_per_batch)
            f = _moe(h, comb, rank, w_moe_in[l // 2].astype(BF16), w_moe_out[l // 2].astype(BF16))
            x2 = _resid(x2, f, ada_l, row1(post_ffn_gain[l]), tiles_per_batch)
    return x2.reshape(batch, seq, d)
```

```python
import functools
import math

import jax
import jax.numpy as jnp
from jax import lax
from jax.experimental import pallas as pl
from jax.experimental.pallas import tpu as pltpu

D_MODEL = 1024
D_RNN = 1024
N_RNN_BLOCKS = 8
RNN_BLOCK = D_RNN // N_RNN_BLOCKS
CONV_WIDTH = 4
LRU_C = 8.0
N_HEADS = 8
N_KV_HEADS = 2
GROUP = N_HEADS // N_KV_HEADS
HEAD_DIM = 128
WINDOW = 128
D_ATTN = N_HEADS * HEAD_DIM
D_KV = N_KV_HEADS * HEAD_DIM
D_GATES = 2 * D_MODEL
D_IN = 2 * D_RNN + D_ATTN + 2 * D_KV + D_GATES
D_FF = 3 * D_MODEL
N_EXPERTS = 8
D_FF_EXPERT = D_FF // 2
EPS = 1e-6

BF16 = jnp.bfloat16
F32 = jnp.float32

SUBLANES = 8
LANES = 128
MIB = 1024 * 1024

ROW_TILE = 512
SEQ_TILE = 512
Q_TILE = 512
PROJ_CHUNK = 512
FF_CHUNK = 512
MOE_SUPER = 2048
MOE_SUB = 256
MOE_PART = 128
MOE_ALIGN = 16
MOE_CHUNK = 256
MASK_VALUE = -1e30
LOG2E = math.log2(math.e)


def _params(semantics, vmem_mib):
    return pltpu.CompilerParams(dimension_semantics=semantics, vmem_limit_bytes=vmem_mib * MIB)


def _resident(shape, index_map):
    return pl.BlockSpec(shape, index_map, pipeline_mode=pl.Buffered(1))


def _rms(x, gain):
    return x * lax.rsqrt(jnp.mean(x * x, axis=-1, keepdims=True) + EPS) * gain


def _gelu_tanh(x):
    c = math.sqrt(2.0 / math.pi)
    return 0.5 * x * (1.0 + jnp.tanh(c * (x + 0.044715 * (x * x * x))))


def _silu(x):
    return x * jax.nn.sigmoid(x)


def _ada_kernel(c_ref, w_ref, b_ref, o_ref):
    c = c_ref[...]
    o_ref[...] = jnp.dot(_silu(c), w_ref[...], preferred_element_type=F32,
                         precision=lax.Precision.HIGHEST) + b_ref[...]


def _ada(c, w_ada, b_ada):
    n_layers, d, n6 = w_ada.shape
    b = c.shape[0]
    c_pad = jnp.zeros((SUBLANES, d), F32).at[:b].set(c)
    nb = 1536
    out = pl.pallas_call(
        _ada_kernel,
        grid=(n_layers, n6 // nb),
        in_specs=[
            pl.BlockSpec((SUBLANES, d), lambda l, j: (0, 0)),
            pl.BlockSpec((None, d, nb), lambda l, j: (l, 0, j)),
            pl.BlockSpec((None, 1, nb), lambda l, j: (l, 0, j)),
        ],
        out_specs=pl.BlockSpec((None, SUBLANES, nb), lambda l, j: (l, 0, j)),
        out_shape=jax.ShapeDtypeStruct((n_layers, SUBLANES, n6), F32),
        compiler_params=_params(("arbitrary", "arbitrary"), 32),
        name="ada",
    )(c_pad, w_ada, b_ada.reshape(n_layers, 1, n6))
    return out[:, :b].reshape(n_layers, b, 1, n6)


def _ada_spec(col, tiles_per_batch, lead=0):
    return pl.BlockSpec((None, 1, D_MODEL),
                        lambda i, *_: (jnp.maximum(i - lead, 0) // tiles_per_batch, 0, col))


W_STEPS = 8


def _row_spec(tm, width):
    return pl.BlockSpec((tm, width), lambda i: (jnp.maximum(i - W_STEPS, 0), 0))


def _weight_chunk_spec(layer, rows, cols):
    return pl.BlockSpec((None, rows // W_STEPS, cols), lambda i: (layer, jnp.minimum(i, W_STEPS - 1), 0))


def _keep_weight_chunk(i, w_ref, w_scr):
    rows = w_ref.shape[0]
    w_scr[pl.ds(pl.multiple_of(i * rows, rows), rows), :] = w_ref[...].astype(BF16)


def _inproj_kernel(x_ref, gain_ref, sc_ref, sh_ref, w_ref, xr_ref, gr_ref, q_ref, kv_ref, gt_ref, w_scr):
    i = pl.program_id(0)

    @pl.when(i < W_STEPS)
    def _():
        _keep_weight_chunk(i, w_ref, w_scr)

    @pl.when(i >= W_STEPS)
    def _():
        h = _rms(x_ref[...], gain_ref[...]) * (1.0 + sc_ref[...]) + sh_ref[...]
        h = h.astype(BF16)
        outs = ((xr_ref, D_RNN), (gr_ref, D_RNN), (q_ref, D_ATTN), (kv_ref, 2 * D_KV), (gt_ref, D_GATES))
        col = 0
        for ref, width in outs:
            for c in range(0, width, PROJ_CHUNK):
                ref[:, c:c + PROJ_CHUNK] = jnp.dot(
                    h, w_scr[:, col + c:col + c + PROJ_CHUNK], preferred_element_type=F32).astype(ref.dtype)
            col += width


def _inproj(x2, gain, ada_l, w_in, layer, tiles_per_batch):
    t = x2.shape[0]
    tm = ROW_TILE
    widths = (D_RNN, D_RNN, D_ATTN, 2 * D_KV, D_GATES)
    return pl.pallas_call(
        _inproj_kernel,
        grid=(W_STEPS + t // tm,),
        in_specs=[
            _row_spec(tm, D_MODEL),
            _resident((1, D_MODEL), lambda i: (0, 0)),
            _ada_spec(1, tiles_per_batch, W_STEPS),
            _ada_spec(0, tiles_per_batch, W_STEPS),
            _weight_chunk_spec(layer, D_MODEL, D_IN),
        ],
        out_specs=[_row_spec(tm, w) for w in widths],
        out_shape=[jax.ShapeDtypeStruct((t, w), BF16) for w in widths],
        scratch_shapes=[pltpu.VMEM((D_MODEL, D_IN), BF16)],
        compiler_params=_params(("arbitrary",), 48),
        name="inproj",
    )(x2, gain, ada_l, ada_l, w_in)


def _rnn_kernel(xr_ref, gr_ref, perm_ref, unperm_ref, cw_ref, cb_ref, wg_ref, ba_ref, bx_ref, lam_ref, o_ref,
                xs, gs, outp, hist, hcar):
    ts = xr_ref.shape[0]
    seg = ts // SUBLANES
    t = pl.program_id(1)

    @pl.when(t == 0)
    def _():
        hist[...] = jnp.zeros((SUBLANES, D_RNN), F32)
        hcar[...] = jnp.zeros((SUBLANES, D_RNN), F32)

    sub = lax.broadcasted_iota(jnp.int32, (SUBLANES, RNN_BLOCK), 0)
    xs[...] = jnp.dot(perm_ref[...], xr_ref[...], preferred_element_type=F32)
    gs[...] = jnp.dot(perm_ref[...], gr_ref[...], preferred_element_type=F32)
    lam = lam_ref[...]
    sp2 = (-LRU_C * LOG2E) * (jnp.maximum(-lam, 0.0) + jnp.log1p(jnp.exp(-jnp.abs(lam))))
    c = math.sqrt(2.0 / math.pi)
    vrow = lambda arr, v: arr[v * SUBLANES:(v + 1) * SUBLANES, :]

    for n in range(N_RNN_BLOCKS):
        cols = slice(n * RNN_BLOCK, (n + 1) * RNN_BLOCK)
        xp = [xs[v * SUBLANES:(v + 1) * SUBLANES, cols] for v in range(seg)]

        def tail(j):
            rolled = pltpu.roll(xp[seg - j], 1, axis=0)
            return jnp.where(sub == 0, jnp.broadcast_to(hist[j:j + 1, cols], (SUBLANES, RNN_BLOCK)), rolled)

        tails = {j: tail(j) for j in range(1, CONV_WIDTH)}
        for j in range(1, CONV_WIDTH):
            hist[j:j + 1, cols] = xp[seg - j][SUBLANES - 1:SUBLANES, :]

        xc = cb_ref[:, cols] + cw_ref[CONV_WIDTH - 1:CONV_WIDTH, cols] * jnp.concatenate(xp, axis=0)
        for j in range(1, CONV_WIDTH):
            k = CONV_WIDTH - 1 - j
            shifted = jnp.concatenate([tails[j - v] for v in range(j)] + xp[:seg - j], axis=0)
            xc = xc + cw_ref[k:k + 1, cols] * shifted

        g = jnp.dot(xc.astype(BF16), wg_ref[n], preferred_element_type=F32)
        r = jax.nn.sigmoid(g[:, :RNN_BLOCK] + ba_ref[:, cols])
        i = jax.nn.sigmoid(g[:, RNN_BLOCK:] + bx_ref[:, cols])
        a = jnp.exp2(r * sp2[:, cols])
        w = 1.0 - a * a
        u = (w * lax.rsqrt(jnp.maximum(w, 1e-30))) * i * xc

        h = jnp.zeros((SUBLANES, RNN_BLOCK), F32)
        prod = jnp.ones((SUBLANES, RNN_BLOCK), F32)
        hs, prods = [], []
        for v in range(seg):
            h = vrow(a, v) * h + vrow(u, v)
            prod = vrow(a, v) * prod
            hs.append(h)
            prods.append(prod)

        pa, pb = prod, h
        for s in (1, 2, 4):
            a_sh = pltpu.roll(pa, s, axis=0)
            b_sh = pltpu.roll(pb, s, axis=0)
            m = sub >= s
            pb = jnp.where(m, pa * b_sh + pb, pb)
            pa = jnp.where(m, pa * a_sh, pa)
        h_end = pb + pa * hcar[:, cols]
        carry_in = jnp.where(sub == 0, hcar[:, cols], pltpu.roll(h_end, 1, axis=0))
        hcar[:, cols] = jnp.broadcast_to(h_end[SUBLANES - 1:SUBLANES, :], (SUBLANES, RNN_BLOCK))

        h_all = jnp.concatenate([hs[v] + prods[v] * carry_in for v in range(seg)], axis=0)

        gg = gs[:, cols]
        th = jnp.tanh(gg * (c + (c * 0.044715) * (gg * gg)))
        hg = h_all * (0.5 * gg)
        outp[:, cols] = (hg + hg * th).astype(BF16)

    o_ref[...] = jnp.dot(unperm_ref[...], outp[...], preferred_element_type=F32).astype(o_ref.dtype)


def _segment_permutation(ts):
    seg = ts // SUBLANES
    p = jnp.arange(ts)
    src = (p % SUBLANES) * seg + p // SUBLANES
    perm = (src[:, None] == jnp.arange(ts)[None, :]).astype(BF16)
    return perm, perm.T


def _rnn(xr, gr, conv_w, conv_b, w_gate, b_a, b_x, lam, batch, seq):
    ts = SEQ_TILE
    nt = seq // ts
    row = lambda b, t: (b * nt + t, 0)
    const2 = lambda b, t: (0, 0)
    perm, unperm = _segment_permutation(ts)
    return pl.pallas_call(
        _rnn_kernel,
        grid=(batch, nt),
        in_specs=[
            pl.BlockSpec((ts, D_RNN), row),
            pl.BlockSpec((ts, D_RNN), row),
            _resident((ts, ts), const2),
            _resident((ts, ts), const2),
            _resident((CONV_WIDTH, D_RNN), const2),
            _resident((1, D_RNN), const2),
            _resident((N_RNN_BLOCKS, RNN_BLOCK, 2 * RNN_BLOCK), lambda b, t: (0, 0, 0)),
            _resident((1, D_RNN), const2),
            _resident((1, D_RNN), const2),
            _resident((1, D_RNN), const2),
        ],
        out_specs=pl.BlockSpec((ts, D_RNN), row),
        out_shape=jax.ShapeDtypeStruct((batch * seq, D_RNN), BF16),
        scratch_shapes=[
            pltpu.VMEM((ts, D_RNN), F32),
            pltpu.VMEM((ts, D_RNN), F32),
            pltpu.VMEM((ts, D_RNN), BF16),
            pltpu.VMEM((SUBLANES, D_RNN), F32),
            pltpu.VMEM((SUBLANES, D_RNN), F32),
        ],
        compiler_params=_params(("arbitrary", "arbitrary"), 40),
        name="rnn",
    )(xr, gr, perm, unperm, conv_w, conv_b, w_gate, b_a, b_x, lam)


def _attn_bias(sinks):
    qi = jnp.arange(WINDOW)[:, None]
    sj = jnp.arange(2 * WINDOW)[None, :]
    dist = (qi + WINDOW - sj).astype(F32)
    valid = (dist >= 0) & (dist < WINDOW)
    slopes = jnp.asarray([2.0 ** (-8.0 * (h + 1) / N_HEADS) for h in range(N_HEADS)], F32)
    bias = jnp.where(valid[None], -slopes[:, None, None] * dist[None], MASK_VALUE)
    bias = jnp.where((sj == 0)[None], sinks.astype(F32)[:, None, None], bias)
    return (bias * LOG2E).reshape(N_KV_HEADS, GROUP * WINDOW, 2 * WINDOW)


def _attn_kernel(q_ref, kv_ref, kvp_ref, bias_ref, o_ref, s_scr, p_scr):
    tq = q_ref.shape[0]
    first = pl.program_id(1) == 0
    col = lax.broadcasted_iota(jnp.int32, (GROUP * WINDOW, 2 * WINDOW), 1)
    first_mask = jnp.where(first & (col >= 1) & (col < WINDOW), MASK_VALUE, 0.0).astype(F32)
    slot0 = lax.broadcasted_iota(jnp.int32, (2 * WINDOW, HEAD_DIM), 0) == 0
    zeros = jnp.zeros((2 * WINDOW, HEAD_DIM), BF16)
    ones = jnp.ones((2 * WINDOW, HEAD_DIM), BF16)
    scale2 = (HEAD_DIM ** -0.5) * LOG2E
    units = [(jb, kvh) for jb in range(tq // WINDOW) for kvh in range(N_KV_HEADS)]

    def keys_or_values(jb, cols):
        r0 = jb * WINDOW
        prev = kvp_ref[:, cols] if jb == 0 else kv_ref[r0 - WINDOW:r0, cols]
        kv = jnp.concatenate([prev, kv_ref[r0:r0 + WINDOW, cols]], axis=0)
        return jnp.where(slot0, zeros, kv)

    for u, (jb, kvh) in enumerate(units):
        r0 = jb * WINDOW
        k = keys_or_values(jb, slice(kvh * HEAD_DIM, (kvh + 1) * HEAD_DIM))
        q = jnp.concatenate(
            [q_ref[r0:r0 + WINDOW, (kvh * GROUP + g) * HEAD_DIM:(kvh * GROUP + g + 1) * HEAD_DIM]
             for g in range(GROUP)], axis=0)
        s = lax.dot_general(q, k, (((1,), (1,)), ((), ())), preferred_element_type=F32)
        s = s * scale2 + bias_ref[kvh]
        if jb == 0:
            s = s + first_mask
        s_scr[u] = s

    for u, (jb, kvh) in enumerate(units):
        r0 = jb * WINDOW
        v = keys_or_values(jb, slice(D_KV + kvh * HEAD_DIM, D_KV + (kvh + 1) * HEAD_DIM))
        v_ext = jnp.concatenate([v, ones], axis=1)
        for g in range(GROUP):
            rows = slice(g * WINDOW, (g + 1) * WINDOW)
            s = s_scr[u, rows, :]
            m = jnp.max(s, axis=-1, keepdims=True)
            p_scr[u, rows, :] = jnp.exp2(s - m).astype(BF16)
        o_ext = jnp.dot(p_scr[u], v_ext, preferred_element_type=F32)
        o = o_ext[:, :HEAD_DIM] / o_ext[:, HEAD_DIM:]
        for g in range(GROUP):
            h = kvh * GROUP + g
            o_ref[r0:r0 + WINDOW, h * HEAD_DIM:(h + 1) * HEAD_DIM] = (
                o[g * WINDOW:(g + 1) * WINDOW, :].astype(o_ref.dtype))


def _attn(q, kv, sinks, batch, seq):
    tq = Q_TILE
    nq = seq // tq
    blocks_per_tile = tq // WINDOW
    blocks_per_seq = seq // WINDOW
    row = lambda b, i: (b * nq + i, 0)
    prev = lambda b, i: (b * blocks_per_seq + jnp.maximum(i * blocks_per_tile - 1, 0), 0)
    return pl.pallas_call(
        _attn_kernel,
        grid=(batch, nq),
        in_specs=[
            pl.BlockSpec((tq, D_ATTN), row),
            pl.BlockSpec((tq, 2 * D_KV), row),
            pl.BlockSpec((WINDOW, 2 * D_KV), prev),
            _resident((N_KV_HEADS, GROUP * WINDOW, 2 * WINDOW), lambda b, i: (0, 0, 0)),
        ],
        out_specs=pl.BlockSpec((tq, D_ATTN), row),
        out_shape=jax.ShapeDtypeStruct((batch * seq, D_ATTN), BF16),
        scratch_shapes=[pltpu.VMEM((blocks_per_tile * N_KV_HEADS, GROUP * WINDOW, 2 * WINDOW), F32),
                        pltpu.VMEM((blocks_per_tile * N_KV_HEADS, GROUP * WINDOW, 2 * WINDOW), BF16)],
        compiler_params=_params(("parallel", "parallel"), 32),
        name="attn",
    )(q, kv, kv, _attn_bias(sinks))


def _mix_kernel(rnn_ref, att_ref, gt_ref, x_ref, gm_ref, gain_ref, wb_ref, wo_ref, o_ref, wb_scr, wo_scr):
    i = pl.program_id(0)

    @pl.when(i < W_STEPS)
    def _():
        _keep_weight_chunk(i, wb_ref, wb_scr)
        _keep_weight_chunk(i, wo_ref, wo_scr)

    @pl.when(i >= W_STEPS)
    def _():
        bp0 = jnp.dot(rnn_ref[...], wb_scr[:D_MODEL, :], preferred_element_type=F32)
        bp1 = jnp.dot(att_ref[...], wb_scr[D_MODEL:, :], preferred_element_type=F32)
        g0 = jax.nn.sigmoid(gt_ref[:, :D_MODEL].astype(F32))
        g1 = jax.nn.sigmoid(gt_ref[:, D_MODEL:].astype(F32))
        merged = (g0 * bp0 + g1 * bp1).astype(BF16)
        mix = jnp.dot(merged, wo_scr[...], preferred_element_type=F32)
        o_ref[...] = x_ref[...] + gm_ref[...] * _rms(mix, gain_ref[...])


def _mix(rnn_out, attn_out, gates, x2, ada_l, gain, w_branch, w_out, layer, tiles_per_batch):
    t = x2.shape[0]
    tm = ROW_TILE
    return pl.pallas_call(
        _mix_kernel,
        grid=(W_STEPS + t // tm,),
        in_specs=[
            _row_spec(tm, D_MODEL),
            _row_spec(tm, D_MODEL),
            _row_spec(tm, D_GATES),
            _row_spec(tm, D_MODEL),
            _ada_spec(2, tiles_per_batch, W_STEPS),
            _resident((1, D_MODEL), lambda i: (0, 0)),
            _weight_chunk_spec(layer, 2 * D_MODEL, D_MODEL),
            _weight_chunk_spec(layer, D_MODEL, D_MODEL),
        ],
        out_specs=_row_spec(tm, D_MODEL),
        out_shape=jax.ShapeDtypeStruct((t, D_MODEL), F32),
        scratch_shapes=[pltpu.VMEM((2 * D_MODEL, D_MODEL), BF16), pltpu.VMEM((D_MODEL, D_MODEL), BF16)],
        compiler_params=_params(("arbitrary",), 40),
        name="mix",
    )(rnn_out, attn_out, gates, x2, ada_l, gain, w_branch.reshape(-1, 2 * D_MODEL, D_MODEL), w_out)


def _ffn_kernel(x_ref, g1_ref, sc_ref, sh_ref, gf_ref, g2_ref, wi_ref, wo_ref, o_ref, wi_scr, wo_scr, acc):
    i = pl.program_id(0)

    @pl.when(i < W_STEPS)
    def _():
        _keep_weight_chunk(i, wi_ref, wi_scr)
        _keep_weight_chunk(i, wo_ref, wo_scr)

    @pl.when(i >= W_STEPS)
    def _():
        x = x_ref[...]
        h = (_rms(x, g1_ref[...]) * (1.0 + sc_ref[...]) + sh_ref[...]).astype(BF16)
        for c in range(0, D_FF, FF_CHUNK):
            gate = jnp.dot(h, wi_scr[:, c:c + FF_CHUNK], preferred_element_type=F32)
            up = jnp.dot(h, wi_scr[:, D_FF + c:D_FF + c + FF_CHUNK], preferred_element_type=F32)
            a = (_silu(gate) * up).astype(BF16)
            part = jnp.dot(a, wo_scr[c:c + FF_CHUNK, :], preferred_element_type=F32)
            if c == 0:
                acc[...] = part
            else:
                acc[...] += part
        o_ref[...] = x + gf_ref[...] * _rms(acc[...], g2_ref[...])


def _ffn(x2, ada_l, gain1, gain2, w_in, w_out, layer, tiles_per_batch):
    t = x2.shape[0]
    tm = ROW_TILE
    return pl.pallas_call(
        _ffn_kernel,
        grid=(W_STEPS + t // tm,),
        in_specs=[
            _row_spec(tm, D_MODEL),
            _resident((1, D_MODEL), lambda i: (0, 0)),
            _ada_spec(4, tiles_per_batch, W_STEPS),
            _ada_spec(3, tiles_per_batch, W_STEPS),
            _ada_spec(5, tiles_per_batch, W_STEPS),
            _resident((1, D_MODEL), lambda i: (0, 0)),
            _weight_chunk_spec(layer, D_MODEL, 2 * D_FF),
            _weight_chunk_spec(layer, D_FF, D_MODEL),
        ],
        out_specs=_row_spec(tm, D_MODEL),
        out_shape=jax.ShapeDtypeStruct((t, D_MODEL), F32),
        scratch_shapes=[pltpu.VMEM((D_MODEL, 2 * D_FF), BF16), pltpu.VMEM((D_FF, D_MODEL), BF16),
                        pltpu.VMEM((tm, D_MODEL), F32)],
        compiler_params=_params(("arbitrary",), 56),
        name="ffn",
    )(x2, gain1, ada_l, ada_l, ada_l, gain2, w_in, w_out)


def _router_kernel(x_ref, g1_ref, sc_ref, sh_ref, wr_ref, h_ref, comb_ref, rank_ref):
    h = _rms(x_ref[...], g1_ref[...]) * (1.0 + sc_ref[...]) + sh_ref[...]
    h_ref[...] = h.astype(h_ref.dtype)
    logits = jnp.dot(h, wr_ref[...], preferred_element_type=F32, precision=lax.Precision.HIGHEST)
    lane = lax.broadcasted_iota(jnp.int32, logits.shape, 1)
    neg = jnp.float32(-jnp.inf)
    lg = jnp.where(lane < N_EXPERTS, logits, neg)
    m1 = jnp.max(lg, axis=-1, keepdims=True)
    i1 = jnp.min(jnp.where(lg == m1, lane, LANES), axis=-1, keepdims=True)
    lg2 = jnp.where(lane == i1, neg, lg)
    m2 = jnp.max(lg2, axis=-1, keepdims=True)
    i2 = jnp.min(jnp.where(lg2 == m2, lane, LANES), axis=-1, keepdims=True)
    e2 = jnp.exp(m2 - m1)
    p1 = 1.0 / (1.0 + e2)
    p2 = e2 / (1.0 + e2)
    comb = jnp.where(lane == i1, p1, 0.0) + jnp.where(lane == i2, p2, 0.0)
    comb_ref[...] = comb
    sel = jnp.where(comb > 0.0, 1.0, 0.0).astype(BF16)
    r = lax.broadcasted_iota(jnp.int32, (MOE_SUB, MOE_SUB), 0)
    c = lax.broadcasted_iota(jnp.int32, (MOE_SUB, MOE_SUB), 1)
    tri = jnp.where(c <= r, 1.0, 0.0).astype(BF16)
    for r0 in range(0, comb.shape[0], MOE_SUB):
        upto = jnp.dot(tri, sel[r0:r0 + MOE_SUB, :], preferred_element_type=F32)
        rank_ref[r0:r0 + MOE_SUB, :] = jnp.where(comb[r0:r0 + MOE_SUB, :] > 0.0, upto - 1.0, -1.0)


def _router(x2, ada_l, gain1, w_router_pad, tiles_per_batch):
    t = x2.shape[0]
    tm = ROW_TILE
    row = lambda i: (i, 0)
    return pl.pallas_call(
        _router_kernel,
        grid=(t // tm,),
        in_specs=[
            pl.BlockSpec((tm, D_MODEL), row),
            _resident((1, D_MODEL), lambda i: (0, 0)),
            _ada_spec(4, tiles_per_batch),
            _ada_spec(3, tiles_per_batch),
            _resident((D_MODEL, LANES), lambda i: (0, 0)),
        ],
        out_specs=[pl.BlockSpec((tm, D_MODEL), row), pl.BlockSpec((tm, LANES), row),
                   pl.BlockSpec((tm, LANES), row)],
        out_shape=[jax.ShapeDtypeStruct((t, D_MODEL), BF16), jax.ShapeDtypeStruct((t, LANES), F32),
                   jax.ShapeDtypeStruct((t, LANES), F32)],
        compiler_params=_params(("parallel",), 32),
        name="router",
    )(x2, gain1, ada_l, ada_l, w_router_pad)


def _moe_kernel(nch_ref, cnt_ref, off_ref, h_ref, srow_ref, scol_ref, comb_ref, wi_ref, wo_ref, o_ref, xg):
    s = pl.program_id(0)
    e = pl.program_id(1)
    n_sub = MOE_SUPER // MOE_SUB
    group = s * N_EXPERTS + e
    n_parts = nch_ref[group]

    @pl.when(e == 0)
    def _():
        o_ref[...] = jnp.zeros(o_ref.shape, F32)

    @pl.when(n_parts > 0)
    def _():
        last = pl.multiple_of((n_parts - 1) * MOE_PART, MOE_PART)
        xg[pl.ds(last, MOE_PART), :] = jnp.zeros((MOE_PART, D_MODEL), BF16)

    dest_col = lax.broadcasted_iota(jnp.int32, (MOE_PART, MOE_SUB), 0).astype(F32)
    dest_row = lax.broadcasted_iota(jnp.int32, (MOE_SUB, MOE_PART), 1).astype(F32)
    lane = lax.broadcasted_iota(jnp.int32, (MOE_SUB, LANES), 1)

    def gathered(j, part):
        srow = srow_ref[:, j * MOE_SUB:(j + 1) * MOE_SUB]
        onehot = jnp.where(srow - float(part * MOE_PART) == dest_col, 1.0, 0.0).astype(BF16)
        rows = jnp.dot(onehot, h_ref[j * MOE_SUB:(j + 1) * MOE_SUB, :], preferred_element_type=F32)
        return rows.astype(BF16)

    for j in range(n_sub):
        start = pl.multiple_of(off_ref[group * n_sub + j], MOE_ALIGN)
        xg[pl.ds(start, MOE_PART), :] = gathered(j, 0)

    part_row = lax.broadcasted_iota(jnp.int32, (MOE_PART, D_MODEL), 0)
    for j in range(n_sub):
        cnt = cnt_ref[group * n_sub + j]
        for part in range(1, MOE_SUB // MOE_PART):
            @pl.when(cnt > part * MOE_PART)
            def _():
                start = pl.multiple_of(off_ref[group * n_sub + j] + part * MOE_PART, MOE_ALIGN)
                cnt_pad = jnp.bitwise_and(cnt + (MOE_ALIGN - 1), -MOE_ALIGN)
                own = part_row < cnt_pad - part * MOE_PART
                xg[pl.ds(start, MOE_PART), :] = jnp.where(own, gathered(j, part), xg[pl.ds(start, MOE_PART), :])

    def swiglu_rows(r0, n_rows):
        x_e = xg[pl.ds(r0, n_rows), :]
        gate = jnp.dot(x_e, wi_ref[:, :D_FF_EXPERT], preferred_element_type=F32)
        up = jnp.dot(x_e, wi_ref[:, D_FF_EXPERT:], preferred_element_type=F32)
        a = (_silu(gate) * up).astype(BF16)
        xg[pl.ds(r0, n_rows), :] = jnp.dot(a, wo_ref[...], preferred_element_type=F32).astype(BF16)

    def chunk(c, carry):
        swiglu_rows(pl.multiple_of(c * MOE_CHUNK, MOE_CHUNK), MOE_CHUNK)
        return carry

    n_full = lax.shift_right_logical(n_parts, 1)
    lax.fori_loop(0, n_full, chunk, 0)

    @pl.when(n_parts % 2 == 1)
    def _():
        swiglu_rows(pl.multiple_of((n_parts - 1) * MOE_PART, MOE_PART), MOE_PART)

    def scatter(j, part):
        rows = slice(j * MOE_SUB, (j + 1) * MOE_SUB)
        sel = lane == e
        scol = jnp.sum(jnp.where(sel, scol_ref[rows, :], 0.0), axis=-1, keepdims=True)
        prob = jnp.sum(jnp.where(sel, comb_ref[rows, :], 0.0), axis=-1, keepdims=True)
        start = pl.multiple_of(off_ref[group * n_sub + j] + part * MOE_PART, MOE_ALIGN)
        onehot = jnp.where(scol - float(part * MOE_PART) == dest_row, 1.0, 0.0).astype(BF16)
        y = xg[pl.ds(start, MOE_PART), :]
        if part > 0:
            cnt_pad = jnp.bitwise_and(cnt_ref[group * n_sub + j] + (MOE_ALIGN - 1), -MOE_ALIGN)
            y = jnp.where(part_row < cnt_pad - part * MOE_PART, y, jnp.zeros_like(y))
        o_ref[rows, :] += prob * jnp.dot(onehot, y, preferred_element_type=F32)

    for j in range(n_sub):
        scatter(j, 0)
    for j in range(n_sub):
        cnt = cnt_ref[group * n_sub + j]
        for part in range(1, MOE_SUB // MOE_PART):
            pl.when(cnt > part * MOE_PART)(functools.partial(scatter, j, part))


def _moe(h, comb, rank, w_in, w_out, layer):
    t = h.shape[0]
    n_super = t // MOE_SUPER
    n_sub = MOE_SUPER // MOE_SUB
    sel = (comb[:, :N_EXPERTS] > 0.0).reshape(n_super, n_sub, MOE_SUB, N_EXPERTS)
    cnt = sel.astype(jnp.int32).sum(axis=2)
    cnt_pad = (cnt + MOE_ALIGN - 1) // MOE_ALIGN * MOE_ALIGN
    off = jnp.cumsum(cnt_pad, axis=1) - cnt_pad
    total = cnt_pad.sum(axis=1)
    rank_rows = (rank[:, :N_EXPERTS].reshape(n_super, MOE_SUPER, N_EXPERTS).transpose(0, 2, 1)
                 .reshape(n_super, N_EXPERTS, 1, MOE_SUPER))
    n_parts = ((total + MOE_PART - 1) // MOE_PART).reshape(-1).astype(jnp.int32)
    by_group = lambda a: a.transpose(0, 2, 1).reshape(-1).astype(jnp.int32)

    xg_rows = -(-(MOE_SUPER + n_sub * (MOE_ALIGN - 1) + MOE_PART) // MOE_CHUNK) * MOE_CHUNK
    once = lambda shape, index_map: pl.BlockSpec(shape, index_map, pipeline_mode=pl.Buffered(1))
    grid_spec = pltpu.PrefetchScalarGridSpec(
        num_scalar_prefetch=3,
        grid=(n_super, N_EXPERTS),
        in_specs=[
            once((MOE_SUPER, D_MODEL), lambda s, e, *_: (s, 0)),
            pl.BlockSpec((None, None, 1, MOE_SUPER), lambda s, e, *_: (s, e, 0, 0)),
            once((MOE_SUPER, LANES), lambda s, e, *_: (s, 0)),
            once((MOE_SUPER, LANES), lambda s, e, *_: (s, 0)),
            pl.BlockSpec((None, D_MODEL, 2 * D_FF_EXPERT), lambda s, e, *_: (layer * N_EXPERTS + e, 0, 0)),
            pl.BlockSpec((None, D_FF_EXPERT, D_MODEL), lambda s, e, *_: (layer * N_EXPERTS + e, 0, 0)),
        ],
        out_specs=pl.BlockSpec((MOE_SUPER, D_MODEL), lambda s, e, *_: (s, 0)),
        scratch_shapes=[pltpu.VMEM((xg_rows, D_MODEL), BF16)],
    )
    return pl.pallas_call(
        _moe_kernel,
        grid_spec=grid_spec,
        out_shape=jax.ShapeDtypeStruct((t, D_MODEL), F32),
        compiler_params=_params(("arbitrary", "arbitrary"), 58),
        name="moe",
    )(n_parts, by_group(cnt), by_group(off), h, rank_rows, rank, comb, w_in, w_out)


def _resid_kernel(x_ref, f_ref, gf_ref, g2_ref, o_ref):
    o_ref[...] = x_ref[...] + gf_ref[...] * _rms(f_ref[...], g2_ref[...])


def _resid(x2, f, ada_l, gain2, tiles_per_batch):
    t = x2.shape[0]
    tm = ROW_TILE
    row = lambda i: (i, 0)
    return pl.pallas_call(
        _resid_kernel,
        grid=(t // tm,),
        in_specs=[
            pl.BlockSpec((tm, D_MODEL), row),
            pl.BlockSpec((tm, D_MODEL), row),
            _ada_spec(5, tiles_per_batch),
            _resident((1, D_MODEL), lambda i: (0, 0)),
        ],
        out_specs=pl.BlockSpec((tm, D_MODEL), row),
        out_shape=jax.ShapeDtypeStruct((t, D_MODEL), F32),
        compiler_params=_params(("parallel",), 32),
        name="resid",
    )(x2, f, ada_l, gain2)


def kernel(x, c, w_ada, b_ada, pre_mix_gain, post_mix_gain, pre_ffn_gain, post_ffn_gain, w_in, conv_w, conv_b,
           w_rg_a, b_rg_a, w_rg_x, b_rg_x, lru_lambda, attn_sinks, w_branch, w_out, w_ffn_in, w_ffn_out,
           w_router, w_moe_in, w_moe_out):
    batch, seq, d = x.shape
    depth = w_in.shape[0]
    t = batch * seq
    tiles_per_batch = seq // ROW_TILE
    row1 = lambda v: v.reshape(1, -1)

    ada = _ada(c, w_ada, b_ada)
    w_moe_in_bf16 = w_moe_in.astype(BF16).reshape(-1, D_MODEL, 2 * D_FF_EXPERT)
    w_moe_out_bf16 = w_moe_out.astype(BF16).reshape(-1, D_FF_EXPERT, D_MODEL)
    x2 = x.reshape(t, d)
    for l in range(depth):
        ada_l = ada[l]
        xr, gr, q, kv, gates = _inproj(x2, row1(pre_mix_gain[l]), ada_l, w_in, l, tiles_per_batch)
        w_gate = jnp.concatenate([w_rg_a[l], w_rg_x[l]], axis=-1).astype(BF16)
        rnn_out = _rnn(xr, gr, conv_w[l], row1(conv_b[l]), w_gate, row1(b_rg_a[l]), row1(b_rg_x[l]),
                       row1(lru_lambda[l]), batch, seq)
        attn_out = _attn(q, kv, attn_sinks[l], batch, seq)
        x2 = _mix(rnn_out, attn_out, gates, x2, ada_l, row1(post_mix_gain[l]), w_branch, w_out, l,
                  tiles_per_batch)
        if l % 2 == 0:
            x2 = _ffn(x2, ada_l, row1(pre_ffn_gain[l]), row1(post_ffn_gain[l]), w_ffn_in, w_ffn_out, l // 2,
                      tiles_per_batch)
        else:
            w_r = jnp.pad(w_router[l // 2], ((0, 0), (0, LANES - N_EXPERTS)))
            h, comb, rank = _router(x2, ada_l, row1(pre_ffn_gain[l]), w_r, tiles_per_batch)
            f = _moe(h, comb, rank, w_moe_in_bf16, w_moe_out_bf16, l // 2)
            x2 = _resid(x2, f, ada_l, row1(post_ffn_gain[l]), tiles_per_batch)
    return x2.reshape(batch, seq, d)
```

```python
import functools
import math

import jax
import jax.numpy as jnp
from jax import lax
from jax.experimental import pallas as pl
from jax.experimental.pallas import tpu as pltpu

D_MODEL = 1024
D_RNN = 1024
N_RNN_BLOCKS = 8
RNN_BLOCK = D_RNN // N_RNN_BLOCKS
CONV_WIDTH = 4
LRU_C = 8.0
N_HEADS = 8
N_KV_HEADS = 2
GROUP = N_HEADS // N_KV_HEADS
HEAD_DIM = 128
WINDOW = 128
D_ATTN = N_HEADS * HEAD_DIM
D_KV = N_KV_HEADS * HEAD_DIM
D_GATES = 2 * D_MODEL
D_IN = 2 * D_RNN + D_ATTN + 2 * D_KV + D_GATES
D_FF = 3 * D_MODEL
N_EXPERTS = 8
D_FF_EXPERT = D_FF // 2
EPS = 1e-6

BF16 = jnp.bfloat16
F32 = jnp.float32

SUBLANES = 8
LANES = 128
MIB = 1024 * 1024

ROW_TILE = 512
SEQ_TILE = 512
Q_TILE = 512
PROJ_CHUNK = 512
FF_CHUNK = 512
MOE_SUPER = 2048
MOE_SUB = 256
MOE_PART = 128
MOE_ALIGN = 16
MOE_CHUNK = 256
MASK_VALUE = -1e30
LOG2E = math.log2(math.e)


def _params(semantics, vmem_mib):
    return pltpu.CompilerParams(dimension_semantics=semantics, vmem_limit_bytes=vmem_mib * MIB)


def _resident(shape, index_map):
    return pl.BlockSpec(shape, index_map, pipeline_mode=pl.Buffered(1))


def _rms(x, gain):
    return x * lax.rsqrt(jnp.mean(x * x, axis=-1, keepdims=True) + EPS) * gain


def _gelu_tanh(x):
    c = math.sqrt(2.0 / math.pi)
    return 0.5 * x * (1.0 + jnp.tanh(c * (x + 0.044715 * (x * x * x))))


def _silu(x):
    return x * jax.nn.sigmoid(x)


def _ada_kernel(c_ref, w_ref, b_ref, o_ref):
    c = c_ref[...]
    o_ref[...] = jnp.dot(_silu(c), w_ref[...], preferred_element_type=F32,
                         precision=lax.Precision.HIGHEST) + b_ref[...]


def _ada(c, w_ada, b_ada):
    n_layers, d, n6 = w_ada.shape
    b = c.shape[0]
    c_pad = jnp.zeros((SUBLANES, d), F32).at[:b].set(c)
    nb = 1536
    out = pl.pallas_call(
        _ada_kernel,
        grid=(n_layers, n6 // nb),
        in_specs=[
            pl.BlockSpec((SUBLANES, d), lambda l, j: (0, 0)),
            pl.BlockSpec((None, d, nb), lambda l, j: (l, 0, j)),
            pl.BlockSpec((None, 1, nb), lambda l, j: (l, 0, j)),
        ],
        out_specs=pl.BlockSpec((None, SUBLANES, nb), lambda l, j: (l, 0, j)),
        out_shape=jax.ShapeDtypeStruct((n_layers, SUBLANES, n6), F32),
        compiler_params=_params(("arbitrary", "arbitrary"), 32),
        name="ada",
    )(c_pad, w_ada, b_ada.reshape(n_layers, 1, n6))
    return out[:, :b].reshape(n_layers, b, 1, n6)


def _ada_spec(col, tiles_per_batch, lead=0):
    return pl.BlockSpec((None, 1, D_MODEL),
                        lambda i, *_: (jnp.maximum(i - lead, 0) // tiles_per_batch, 0, col))


W_STEPS = 8


def _row_spec(tm, width):
    return pl.BlockSpec((tm, width), lambda i: (jnp.maximum(i - W_STEPS, 0), 0))


def _weight_chunk_spec(layer, rows, cols):
    return pl.BlockSpec((None, rows // W_STEPS, cols), lambda i: (layer, jnp.minimum(i, W_STEPS - 1), 0))


def _keep_weight_chunk(i, w_ref, w_scr):
    rows = w_ref.shape[0]
    w_scr[pl.ds(pl.multiple_of(i * rows, rows), rows), :] = w_ref[...].astype(BF16)


def _inproj_kernel(x_ref, gain_ref, sc_ref, sh_ref, w_ref, xr_ref, gr_ref, q_ref, kv_ref, gt_ref, w_scr):
    i = pl.program_id(0)

    @pl.when(i < W_STEPS)
    def _():
        _keep_weight_chunk(i, w_ref, w_scr)

    @pl.when(i >= W_STEPS)
    def _():
        h = _rms(x_ref[...], gain_ref[...]) * (1.0 + sc_ref[...]) + sh_ref[...]
        h = h.astype(BF16)
        outs = ((xr_ref, D_RNN), (gr_ref, D_RNN), (q_ref, D_ATTN), (kv_ref, 2 * D_KV), (gt_ref, D_GATES))
        col = 0
        for ref, width in outs:
            for c in range(0, width, PROJ_CHUNK):
                ref[:, c:c + PROJ_CHUNK] = jnp.dot(
                    h, w_scr[:, col + c:col + c + PROJ_CHUNK], preferred_element_type=F32).astype(ref.dtype)
            col += width


def _inproj(x2, gain, ada_l, w_in, layer, tiles_per_batch):
    t = x2.shape[0]
    tm = ROW_TILE
    widths = (D_RNN, D_RNN, D_ATTN, 2 * D_KV, D_GATES)
    return pl.pallas_call(
        _inproj_kernel,
        grid=(W_STEPS + t // tm,),
        in_specs=[
            _row_spec(tm, D_MODEL),
            _resident((1, D_MODEL), lambda i: (0, 0)),
            _ada_spec(1, tiles_per_batch, W_STEPS),
            _ada_spec(0, tiles_per_batch, W_STEPS),
            _weight_chunk_spec(layer, D_MODEL, D_IN),
        ],
        out_specs=[_row_spec(tm, w) for w in widths],
        out_shape=[jax.ShapeDtypeStruct((t, w), BF16) for w in widths],
        scratch_shapes=[pltpu.VMEM((D_MODEL, D_IN), BF16)],
        compiler_params=_params(("arbitrary",), 48),
        name="inproj",
    )(x2, gain, ada_l, ada_l, w_in)


def _rnn_kernel(xr_ref, gr_ref, perm_ref, unperm_ref, cw_ref, cb_ref, wg_ref, ba_ref, bx_ref, lam_ref, o_ref,
                xs, gs, outp, hist, hcar):
    ts = xr_ref.shape[0]
    seg = ts // SUBLANES
    t = pl.program_id(1)

    @pl.when(t == 0)
    def _():
        hist[...] = jnp.zeros((SUBLANES, D_RNN), F32)
        hcar[...] = jnp.zeros((SUBLANES, D_RNN), F32)

    sub = lax.broadcasted_iota(jnp.int32, (SUBLANES, RNN_BLOCK), 0)
    xs[...] = jnp.dot(perm_ref[...], xr_ref[...], preferred_element_type=F32)
    gs[...] = jnp.dot(perm_ref[...], gr_ref[...], preferred_element_type=F32)
    lam = lam_ref[...]
    sp2 = (-LRU_C * LOG2E) * (jnp.maximum(-lam, 0.0) + jnp.log1p(jnp.exp(-jnp.abs(lam))))
    c = math.sqrt(2.0 / math.pi)
    vrow = lambda arr, v: arr[v * SUBLANES:(v + 1) * SUBLANES, :]

    for n in range(N_RNN_BLOCKS):
        cols = slice(n * RNN_BLOCK, (n + 1) * RNN_BLOCK)
        xp = [xs[v * SUBLANES:(v + 1) * SUBLANES, cols] for v in range(seg)]

        def tail(j):
            rolled = pltpu.roll(xp[seg - j], 1, axis=0)
            return jnp.where(sub == 0, jnp.broadcast_to(hist[j:j + 1, cols], (SUBLANES, RNN_BLOCK)), rolled)

        tails = {j: tail(j) for j in range(1, CONV_WIDTH)}
        for j in range(1, CONV_WIDTH):
            hist[j:j + 1, cols] = xp[seg - j][SUBLANES - 1:SUBLANES, :]

        xc = cb_ref[:, cols] + cw_ref[CONV_WIDTH - 1:CONV_WIDTH, cols] * jnp.concatenate(xp, axis=0)
        for j in range(1, CONV_WIDTH):
            k = CONV_WIDTH - 1 - j
            shifted = jnp.concatenate([tails[j - v] for v in range(j)] + xp[:seg - j], axis=0)
            xc = xc + cw_ref[k:k + 1, cols] * shifted

        g = jnp.dot(xc.astype(BF16), wg_ref[n], preferred_element_type=F32)
        r = jax.nn.sigmoid(g[:, :RNN_BLOCK] + ba_ref[:, cols])
        i = jax.nn.sigmoid(g[:, RNN_BLOCK:] + bx_ref[:, cols])
        a = jnp.exp2(r * sp2[:, cols])
        w = 1.0 - a * a
        u = (w * lax.rsqrt(jnp.maximum(w, 1e-30))) * i * xc

        h = jnp.zeros((SUBLANES, RNN_BLOCK), F32)
        prod = jnp.ones((SUBLANES, RNN_BLOCK), F32)
        hs, prods = [], []
        for v in range(seg):
            h = vrow(a, v) * h + vrow(u, v)
            prod = vrow(a, v) * prod
            hs.append(h)
            prods.append(prod)

        pa, pb = prod, h
        for s in (1, 2, 4):
            a_sh = pltpu.roll(pa, s, axis=0)
            b_sh = pltpu.roll(pb, s, axis=0)
            m = sub >= s
            pb = jnp.where(m, pa * b_sh + pb, pb)
            pa = jnp.where(m, pa * a_sh, pa)
        h_end = pb + pa * hcar[:, cols]
        carry_in = jnp.where(sub == 0, hcar[:, cols], pltpu.roll(h_end, 1, axis=0))
        hcar[:, cols] = jnp.broadcast_to(h_end[SUBLANES - 1:SUBLANES, :], (SUBLANES, RNN_BLOCK))

        h_all = jnp.concatenate([hs[v] + prods[v] * carry_in for v in range(seg)], axis=0)

        gg = gs[:, cols]
        th = jnp.tanh(gg * (c + (c * 0.044715) * (gg * gg)))
        hg = h_all * (0.5 * gg)
        outp[:, cols] = (hg + hg * th).astype(BF16)

    o_ref[...] = jnp.dot(unperm_ref[...], outp[...], preferred_element_type=F32).astype(o_ref.dtype)


def _segment_permutation(ts):
    seg = ts // SUBLANES
    p = jnp.arange(ts)
    src = (p % SUBLANES) * seg + p // SUBLANES
    perm = (src[:, None] == jnp.arange(ts)[None, :]).astype(BF16)
    return perm, perm.T


def _rnn(xr, gr, conv_w, conv_b, w_gate, b_a, b_x, lam, batch, seq):
    ts = SEQ_TILE
    nt = seq // ts
    row = lambda b, t: (b * nt + t, 0)
    const2 = lambda b, t: (0, 0)
    perm, unperm = _segment_permutation(ts)
    return pl.pallas_call(
        _rnn_kernel,
        grid=(batch, nt),
        in_specs=[
            pl.BlockSpec((ts, D_RNN), row),
            pl.BlockSpec((ts, D_RNN), row),
            _resident((ts, ts), const2),
            _resident((ts, ts), const2),
            _resident((CONV_WIDTH, D_RNN), const2),
            _resident((1, D_RNN), const2),
            _resident((N_RNN_BLOCKS, RNN_BLOCK, 2 * RNN_BLOCK), lambda b, t: (0, 0, 0)),
            _resident((1, D_RNN), const2),
            _resident((1, D_RNN), const2),
            _resident((1, D_RNN), const2),
        ],
        out_specs=pl.BlockSpec((ts, D_RNN), row),
        out_shape=jax.ShapeDtypeStruct((batch * seq, D_RNN), BF16),
        scratch_shapes=[
            pltpu.VMEM((ts, D_RNN), F32),
            pltpu.VMEM((ts, D_RNN), F32),
            pltpu.VMEM((ts, D_RNN), BF16),
            pltpu.VMEM((SUBLANES, D_RNN), F32),
            pltpu.VMEM((SUBLANES, D_RNN), F32),
        ],
        compiler_params=_params(("arbitrary", "arbitrary"), 40),
        name="rnn",
    )(xr, gr, perm, unperm, conv_w, conv_b, w_gate, b_a, b_x, lam)


def _inrnn_kernel(x_ref, gain_ref, sc_ref, sh_ref, w_ref, perm_ref, unperm_ref, cw_ref, cb_ref, wg_ref, ba_ref,
                  bx_ref, lam_ref, o_ref, q_ref, kv_ref, gt_ref, w_scr, xs, gs, outp, hist, hcar, *, tiles_per_batch):
    step = pl.program_id(0)

    @pl.when(step < W_STEPS)
    def _():
        _keep_weight_chunk(step, w_ref, w_scr)

    @pl.when(step >= W_STEPS)
    def _():
        ts = x_ref.shape[0]
        seg = ts // SUBLANES

        @pl.when((step - W_STEPS) % tiles_per_batch == 0)
        def _():
            hist[...] = jnp.zeros((SUBLANES, D_RNN), F32)
            hcar[...] = jnp.zeros((SUBLANES, D_RNN), F32)

        h_in = (_rms(x_ref[...], gain_ref[...]) * (1.0 + sc_ref[...]) + sh_ref[...]).astype(BF16)
        hp = jnp.dot(perm_ref[...], h_in, preferred_element_type=F32).astype(BF16)
        xs[...] = jnp.dot(hp, w_scr[:, :D_RNN], preferred_element_type=F32)
        gs[...] = jnp.dot(hp, w_scr[:, D_RNN:2 * D_RNN], preferred_element_type=F32)
        proj_chunks = []
        col = 2 * D_RNN
        for ref, width in ((q_ref, D_ATTN), (kv_ref, 2 * D_KV), (gt_ref, D_GATES)):
            proj_chunks += [(ref, c, col + c) for c in range(0, width, PROJ_CHUNK)]
            col += width

        def project(ref, c, wcol):
            ref[:, c:c + PROJ_CHUNK] = jnp.dot(
                h_in, w_scr[:, wcol:wcol + PROJ_CHUNK], preferred_element_type=F32).astype(ref.dtype)

        sub = lax.broadcasted_iota(jnp.int32, (SUBLANES, RNN_BLOCK), 0)
        lam = lam_ref[...]
        sp2 = (-LRU_C * LOG2E) * (jnp.maximum(-lam, 0.0) + jnp.log1p(jnp.exp(-jnp.abs(lam))))
        c = math.sqrt(2.0 / math.pi)
        vrow = lambda arr, v: arr[v * SUBLANES:(v + 1) * SUBLANES, :]

        for n in range(N_RNN_BLOCKS):
            cols = slice(n * RNN_BLOCK, (n + 1) * RNN_BLOCK)
            xp = [xs[v * SUBLANES:(v + 1) * SUBLANES, cols] for v in range(seg)]

            def tail(j):
                rolled = pltpu.roll(xp[seg - j], 1, axis=0)
                return jnp.where(sub == 0, jnp.broadcast_to(hist[j:j + 1, cols], (SUBLANES, RNN_BLOCK)), rolled)

            tails = {j: tail(j) for j in range(1, CONV_WIDTH)}
            for j in range(1, CONV_WIDTH):
                hist[j:j + 1, cols] = xp[seg - j][SUBLANES - 1:SUBLANES, :]

            xc = cb_ref[:, cols] + cw_ref[CONV_WIDTH - 1:CONV_WIDTH, cols] * jnp.concatenate(xp, axis=0)
            for j in range(1, CONV_WIDTH):
                k = CONV_WIDTH - 1 - j
                shifted = jnp.concatenate([tails[j - v] for v in range(j)] + xp[:seg - j], axis=0)
                xc = xc + cw_ref[k:k + 1, cols] * shifted

            g = jnp.dot(xc.astype(BF16), wg_ref[n], preferred_element_type=F32)
            r = jax.nn.sigmoid(g[:, :RNN_BLOCK] + ba_ref[:, cols])
            i = jax.nn.sigmoid(g[:, RNN_BLOCK:] + bx_ref[:, cols])
            a = jnp.exp2(r * sp2[:, cols])
            w = 1.0 - a * a
            u = (w * lax.rsqrt(jnp.maximum(w, 1e-30))) * i * xc

            h = jnp.zeros((SUBLANES, RNN_BLOCK), F32)
            prod = jnp.ones((SUBLANES, RNN_BLOCK), F32)
            hs, prods = [], []
            for v in range(seg):
                h = vrow(a, v) * h + vrow(u, v)
                prod = vrow(a, v) * prod
                hs.append(h)
                prods.append(prod)

            pa, pb = prod, h
            for s in (1, 2, 4):
                a_sh = pltpu.roll(pa, s, axis=0)
                b_sh = pltpu.roll(pb, s, axis=0)
                m = sub >= s
                pb = jnp.where(m, pa * b_sh + pb, pb)
                pa = jnp.where(m, pa * a_sh, pa)
            h_end = pb + pa * hcar[:, cols]
            carry_in = jnp.where(sub == 0, hcar[:, cols], pltpu.roll(h_end, 1, axis=0))
            hcar[:, cols] = jnp.broadcast_to(h_end[SUBLANES - 1:SUBLANES, :], (SUBLANES, RNN_BLOCK))

            h_all = jnp.concatenate([hs[v] + prods[v] * carry_in for v in range(seg)], axis=0)

            gg = gs[:, cols]
            th = jnp.tanh(gg * (c + (c * 0.044715) * (gg * gg)))
            hg = h_all * (0.5 * gg)
            outp[:, cols] = (hg + hg * th).astype(BF16)
            if n < len(proj_chunks):
                project(*proj_chunks[n])

        for chunk in proj_chunks[N_RNN_BLOCKS:]:
            project(*chunk)
        o_ref[...] = jnp.dot(unperm_ref[...], outp[...], preferred_element_type=F32).astype(o_ref.dtype)


def _inrnn(x2, gain, ada_l, w_in, layer, conv_w, conv_b, w_gate, b_a, b_x, lam, tiles_per_batch):
    t = x2.shape[0]
    tm = ROW_TILE
    widths = (D_RNN, D_ATTN, 2 * D_KV, D_GATES)
    perm, unperm = _segment_permutation(tm)
    const2 = lambda i: (0, 0)
    return pl.pallas_call(
        functools.partial(_inrnn_kernel, tiles_per_batch=tiles_per_batch),
        grid=(W_STEPS + t // tm,),
        in_specs=[
            _row_spec(tm, D_MODEL),
            _resident((1, D_MODEL), const2),
            _ada_spec(1, tiles_per_batch, W_STEPS),
            _ada_spec(0, tiles_per_batch, W_STEPS),
            _weight_chunk_spec(layer, D_MODEL, D_IN),
            _resident((tm, tm), const2),
            _resident((tm, tm), const2),
            _resident((CONV_WIDTH, D_RNN), const2),
            _resident((1, D_RNN), const2),
            _resident((N_RNN_BLOCKS, RNN_BLOCK, 2 * RNN_BLOCK), lambda i: (0, 0, 0)),
            _resident((1, D_RNN), const2),
            _resident((1, D_RNN), const2),
            _resident((1, D_RNN), const2),
        ],
        out_specs=[_row_spec(tm, w) for w in widths],
        out_shape=[jax.ShapeDtypeStruct((t, w), BF16) for w in widths],
        scratch_shapes=[
            pltpu.VMEM((D_MODEL, D_IN), BF16),
            pltpu.VMEM((tm, D_RNN), F32),
            pltpu.VMEM((tm, D_RNN), F32),
            pltpu.VMEM((tm, D_RNN), BF16),
            pltpu.VMEM((SUBLANES, D_RNN), F32),
            pltpu.VMEM((SUBLANES, D_RNN), F32),
        ],
        compiler_params=_params(("arbitrary",), 52),
        name="inrnn",
    )(x2, gain, ada_l, ada_l, w_in, perm, unperm, conv_w, conv_b, w_gate, b_a, b_x, lam)


def _attn_bias(sinks):
    qi = jnp.arange(WINDOW)[:, None]
    sj = jnp.arange(2 * WINDOW)[None, :]
    dist = (qi + WINDOW - sj).astype(F32)
    valid = (dist >= 0) & (dist < WINDOW)
    slopes = jnp.asarray([2.0 ** (-8.0 * (h + 1) / N_HEADS) for h in range(N_HEADS)], F32)
    bias = jnp.where(valid[None], -slopes[:, None, None] * dist[None], MASK_VALUE)
    bias = jnp.where((sj == 0)[None], sinks.astype(F32)[:, None, None], bias)
    return (bias * LOG2E).reshape(N_KV_HEADS, GROUP * WINDOW, 2 * WINDOW)


def _attn_kernel(q_ref, kv_ref, kvp_ref, bias_ref, o_ref, s_scr, p_scr):
    tq = q_ref.shape[0]
    first = pl.program_id(1) == 0
    col = lax.broadcasted_iota(jnp.int32, (GROUP * WINDOW, 2 * WINDOW), 1)
    first_mask = jnp.where(first & (col >= 1) & (col < WINDOW), MASK_VALUE, 0.0).astype(F32)
    slot0 = lax.broadcasted_iota(jnp.int32, (2 * WINDOW, HEAD_DIM), 0) == 0
    zeros = jnp.zeros((2 * WINDOW, HEAD_DIM), BF16)
    ones = jnp.ones((2 * WINDOW, HEAD_DIM), BF16)
    scale2 = (HEAD_DIM ** -0.5) * LOG2E
    units = [(jb, kvh) for jb in range(tq // WINDOW) for kvh in range(N_KV_HEADS)]

    def keys_or_values(jb, cols):
        r0 = jb * WINDOW
        prev = kvp_ref[:, cols] if jb == 0 else kv_ref[r0 - WINDOW:r0, cols]
        kv = jnp.concatenate([prev, kv_ref[r0:r0 + WINDOW, cols]], axis=0)
        return jnp.where(slot0, zeros, kv)

    for u, (jb, kvh) in enumerate(units):
        r0 = jb * WINDOW
        k = keys_or_values(jb, slice(kvh * HEAD_DIM, (kvh + 1) * HEAD_DIM))
        q = jnp.concatenate(
            [q_ref[r0:r0 + WINDOW, (kvh * GROUP + g) * HEAD_DIM:(kvh * GROUP + g + 1) * HEAD_DIM]
             for g in range(GROUP)], axis=0)
        s = lax.dot_general(q, k, (((1,), (1,)), ((), ())), preferred_element_type=F32)
        s = s * scale2 + bias_ref[kvh]
        if jb == 0:
            s = s + first_mask
        s_scr[u] = s

    for u, (jb, kvh) in enumerate(units):
        r0 = jb * WINDOW
        v = keys_or_values(jb, slice(D_KV + kvh * HEAD_DIM, D_KV + (kvh + 1) * HEAD_DIM))
        v_ext = jnp.concatenate([v, ones], axis=1)
        for g in range(GROUP):
            rows = slice(g * WINDOW, (g + 1) * WINDOW)
            s = s_scr[u, rows, :]
            m = jnp.max(s, axis=-1, keepdims=True)
            p_scr[u, rows, :] = jnp.exp2(s - m).astype(BF16)
        o_ext = jnp.dot(p_scr[u], v_ext, preferred_element_type=F32)
        o = o_ext[:, :HEAD_DIM] / o_ext[:, HEAD_DIM:]
        for g in range(GROUP):
            h = kvh * GROUP + g
            o_ref[r0:r0 + WINDOW, h * HEAD_DIM:(h + 1) * HEAD_DIM] = (
                o[g * WINDOW:(g + 1) * WINDOW, :].astype(o_ref.dtype))


def _attn(q, kv, sinks, batch, seq):
    tq = Q_TILE
    nq = seq // tq
    blocks_per_tile = tq // WINDOW
    blocks_per_seq = seq // WINDOW
    row = lambda b, i: (b * nq + i, 0)
    prev = lambda b, i: (b * blocks_per_seq + jnp.maximum(i * blocks_per_tile - 1, 0), 0)
    return pl.pallas_call(
        _attn_kernel,
        grid=(batch, nq),
        in_specs=[
            pl.BlockSpec((tq, D_ATTN), row),
            pl.BlockSpec((tq, 2 * D_KV), row),
            pl.BlockSpec((WINDOW, 2 * D_KV), prev),
            _resident((N_KV_HEADS, GROUP * WINDOW, 2 * WINDOW), lambda b, i: (0, 0, 0)),
        ],
        out_specs=pl.BlockSpec((tq, D_ATTN), row),
        out_shape=jax.ShapeDtypeStruct((batch * seq, D_ATTN), BF16),
        scratch_shapes=[pltpu.VMEM((blocks_per_tile * N_KV_HEADS, GROUP * WINDOW, 2 * WINDOW), F32),
                        pltpu.VMEM((blocks_per_tile * N_KV_HEADS, GROUP * WINDOW, 2 * WINDOW), BF16)],
        compiler_params=_params(("parallel", "parallel"), 32),
        name="attn",
    )(q, kv, kv, _attn_bias(sinks))


def _mix_kernel(rnn_ref, att_ref, gt_ref, x_ref, gm_ref, gain_ref, wb_ref, wo_ref, o_ref, wb_scr, wo_scr):
    i = pl.program_id(0)

    @pl.when(i < W_STEPS)
    def _():
        _keep_weight_chunk(i, wb_ref, wb_scr)
        _keep_weight_chunk(i, wo_ref, wo_scr)

    @pl.when(i >= W_STEPS)
    def _():
        bp0 = jnp.dot(rnn_ref[...], wb_scr[:D_MODEL, :], preferred_element_type=F32)
        bp1 = jnp.dot(att_ref[...], wb_scr[D_MODEL:, :], preferred_element_type=F32)
        g0 = jax.nn.sigmoid(gt_ref[:, :D_MODEL].astype(F32))
        g1 = jax.nn.sigmoid(gt_ref[:, D_MODEL:].astype(F32))
        merged = (g0 * bp0 + g1 * bp1).astype(BF16)
        mix = jnp.dot(merged, wo_scr[...], preferred_element_type=F32)
        o_ref[...] = x_ref[...] + gm_ref[...] * _rms(mix, gain_ref[...])


def _mix(rnn_out, attn_out, gates, x2, ada_l, gain, w_branch, w_out, layer, tiles_per_batch):
    t = x2.shape[0]
    tm = ROW_TILE
    return pl.pallas_call(
        _mix_kernel,
        grid=(W_STEPS + t // tm,),
        in_specs=[
            _row_spec(tm, D_MODEL),
            _row_spec(tm, D_MODEL),
            _row_spec(tm, D_GATES),
            _row_spec(tm, D_MODEL),
            _ada_spec(2, tiles_per_batch, W_STEPS),
            _resident((1, D_MODEL), lambda i: (0, 0)),
            _weight_chunk_spec(layer, 2 * D_MODEL, D_MODEL),
            _weight_chunk_spec(layer, D_MODEL, D_MODEL),
        ],
        out_specs=_row_spec(tm, D_MODEL),
        out_shape=jax.ShapeDtypeStruct((t, D_MODEL), F32),
        scratch_shapes=[pltpu.VMEM((2 * D_MODEL, D_MODEL), BF16), pltpu.VMEM((D_MODEL, D_MODEL), BF16)],
        compiler_params=_params(("arbitrary",), 40),
        name="mix",
    )(rnn_out, attn_out, gates, x2, ada_l, gain, w_branch.reshape(-1, 2 * D_MODEL, D_MODEL), w_out)


def _attnmix_kernel(q_ref, kv_ref, kvp_ref, bias_ref, rnn_ref, gt_ref, x_ref, gm_ref, gain_ref, wb_ref, wo_ref,
                    o_ref, wb_scr, wo_scr, s_scr, p_scr, att_scr, m_scr, *, tiles_per_batch):
    step = pl.program_id(0)

    @pl.when(step < W_STEPS)
    def _():
        _keep_weight_chunk(step, wb_ref, wb_scr)
        _keep_weight_chunk(step, wo_ref, wo_scr)

    @pl.when(step >= W_STEPS)
    def _():
        tq = q_ref.shape[0]
        first = (step - W_STEPS) % tiles_per_batch == 0
        col = lax.broadcasted_iota(jnp.int32, (GROUP * WINDOW, 2 * WINDOW), 1)
        first_mask = jnp.where(first & (col >= 1) & (col < WINDOW), MASK_VALUE, 0.0).astype(F32)
        slot0 = lax.broadcasted_iota(jnp.int32, (2 * WINDOW, HEAD_DIM), 0) == 0
        zeros = jnp.zeros((2 * WINDOW, HEAD_DIM), BF16)
        ones = jnp.ones((2 * WINDOW, HEAD_DIM), BF16)
        scale2 = (HEAD_DIM ** -0.5) * LOG2E
        units = [(jb, kvh) for jb in range(tq // WINDOW) for kvh in range(N_KV_HEADS)]

        def keys_or_values(jb, cols):
            r0 = jb * WINDOW
            prev = kvp_ref[:, cols] if jb == 0 else kv_ref[r0 - WINDOW:r0, cols]
            kv = jnp.concatenate([prev, kv_ref[r0:r0 + WINDOW, cols]], axis=0)
            return jnp.where(slot0, zeros, kv)

        for u, (jb, kvh) in enumerate(units):
            r0 = jb * WINDOW
            k = keys_or_values(jb, slice(kvh * HEAD_DIM, (kvh + 1) * HEAD_DIM))
            q = jnp.concatenate(
                [q_ref[r0:r0 + WINDOW, (kvh * GROUP + g) * HEAD_DIM:(kvh * GROUP + g + 1) * HEAD_DIM]
                 for g in range(GROUP)], axis=0)
            s = lax.dot_general(q, k, (((1,), (1,)), ((), ())), preferred_element_type=F32)
            s = s * scale2 + bias_ref[kvh]
            if jb == 0:
                s = s + first_mask
            s_scr[u] = s

        n_chunks = len(units) // 2
        width = D_MODEL // n_chunks

        def recurrent_branch(c):
            cols = slice(c * width, (c + 1) * width)
            bp0 = jnp.dot(rnn_ref[...], wb_scr[:D_MODEL, cols], preferred_element_type=F32)
            m_scr[:, cols] = jax.nn.sigmoid(gt_ref[:, cols].astype(F32)) * bp0

        for u, (jb, kvh) in enumerate(units):
            r0 = jb * WINDOW
            v = keys_or_values(jb, slice(D_KV + kvh * HEAD_DIM, D_KV + (kvh + 1) * HEAD_DIM))
            v_ext = jnp.concatenate([v, ones], axis=1)
            for g in range(GROUP):
                rows = slice(g * WINDOW, (g + 1) * WINDOW)
                s = s_scr[u, rows, :]
                m = jnp.max(s, axis=-1, keepdims=True)
                p_scr[u, rows, :] = jnp.exp2(s - m).astype(BF16)
            o_ext = jnp.dot(p_scr[u], v_ext, preferred_element_type=F32)
            o = o_ext[:, :HEAD_DIM] / o_ext[:, HEAD_DIM:]
            for g in range(GROUP):
                hd = kvh * GROUP + g
                att_scr[r0:r0 + WINDOW, hd * HEAD_DIM:(hd + 1) * HEAD_DIM] = (
                    o[g * WINDOW:(g + 1) * WINDOW, :].astype(BF16))
            if u % 2 == 1:
                recurrent_branch(u // 2)

        bp1 = jnp.dot(att_scr[...], wb_scr[D_MODEL:, :], preferred_element_type=F32)
        g1 = jax.nn.sigmoid(gt_ref[:, D_MODEL:].astype(F32))
        merged = (m_scr[...] + g1 * bp1).astype(BF16)
        mix = jnp.dot(merged, wo_scr[...], preferred_element_type=F32)
        o_ref[...] = x_ref[...] + gm_ref[...] * _rms(mix, gain_ref[...])


def _attnmix(q, kv, sinks, rnn_out, gates, x2, ada_l, gain, w_branch, w_out, layer, tiles_per_batch):
    t = x2.shape[0]
    tm = ROW_TILE
    blocks_per_tile = tm // WINDOW
    n_units = blocks_per_tile * N_KV_HEADS

    def prev_block(i):
        r = jnp.maximum(i - W_STEPS, 0)
        return (r * blocks_per_tile - jnp.where(r % tiles_per_batch == 0, 0, 1), 0)

    return pl.pallas_call(
        functools.partial(_attnmix_kernel, tiles_per_batch=tiles_per_batch),
        grid=(W_STEPS + t // tm,),
        in_specs=[
            _row_spec(tm, D_ATTN),
            _row_spec(tm, 2 * D_KV),
            pl.BlockSpec((WINDOW, 2 * D_KV), prev_block),
            _resident((N_KV_HEADS, GROUP * WINDOW, 2 * WINDOW), lambda i: (0, 0, 0)),
            _row_spec(tm, D_MODEL),
            _row_spec(tm, D_GATES),
            _row_spec(tm, D_MODEL),
            _ada_spec(2, tiles_per_batch, W_STEPS),
            _resident((1, D_MODEL), lambda i: (0, 0)),
            _weight_chunk_spec(layer, 2 * D_MODEL, D_MODEL),
            _weight_chunk_spec(layer, D_MODEL, D_MODEL),
        ],
        out_specs=_row_spec(tm, D_MODEL),
        out_shape=jax.ShapeDtypeStruct((t, D_MODEL), F32),
        scratch_shapes=[
            pltpu.VMEM((2 * D_MODEL, D_MODEL), BF16),
            pltpu.VMEM((D_MODEL, D_MODEL), BF16),
            pltpu.VMEM((n_units, GROUP * WINDOW, 2 * WINDOW), F32),
            pltpu.VMEM((n_units, GROUP * WINDOW, 2 * WINDOW), BF16),
            pltpu.VMEM((tm, D_ATTN), BF16),
            pltpu.VMEM((tm, D_MODEL), F32),
        ],
        compiler_params=_params(("arbitrary",), 48),
        name="attnmix",
    )(q, kv, kv, _attn_bias(sinks), rnn_out, gates, x2, ada_l, gain,
      w_branch.reshape(-1, 2 * D_MODEL, D_MODEL), w_out)


def _ffn_kernel(x_ref, g1_ref, sc_ref, sh_ref, gf_ref, g2_ref, wi_ref, wo_ref, o_ref, wi_scr, wo_scr, acc):
    i = pl.program_id(0)

    @pl.when(i < W_STEPS)
    def _():
        _keep_weight_chunk(i, wi_ref, wi_scr)
        _keep_weight_chunk(i, wo_ref, wo_scr)

    @pl.when(i >= W_STEPS)
    def _():
        x = x_ref[...]
        h = (_rms(x, g1_ref[...]) * (1.0 + sc_ref[...]) + sh_ref[...]).astype(BF16)
        for c in range(0, D_FF, FF_CHUNK):
            gate = jnp.dot(h, wi_scr[:, c:c + FF_CHUNK], preferred_element_type=F32)
            up = jnp.dot(h, wi_scr[:, D_FF + c:D_FF + c + FF_CHUNK], preferred_element_type=F32)
            a = (_silu(gate) * up).astype(BF16)
            part = jnp.dot(a, wo_scr[c:c + FF_CHUNK, :], preferred_element_type=F32)
            if c == 0:
                acc[...] = part
            else:
                acc[...] += part
        o_ref[...] = x + gf_ref[...] * _rms(acc[...], g2_ref[...])


def _ffn(x2, ada_l, gain1, gain2, w_in, w_out, layer, tiles_per_batch):
    t = x2.shape[0]
    tm = ROW_TILE
    return pl.pallas_call(
        _ffn_kernel,
        grid=(W_STEPS + t // tm,),
        in_specs=[
            _row_spec(tm, D_MODEL),
            _resident((1, D_MODEL), lambda i: (0, 0)),
            _ada_spec(4, tiles_per_batch, W_STEPS),
            _ada_spec(3, tiles_per_batch, W_STEPS),
            _ada_spec(5, tiles_per_batch, W_STEPS),
            _resident((1, D_MODEL), lambda i: (0, 0)),
            _weight_chunk_spec(layer, D_MODEL, 2 * D_FF),
            _weight_chunk_spec(layer, D_FF, D_MODEL),
        ],
        out_specs=_row_spec(tm, D_MODEL),
        out_shape=jax.ShapeDtypeStruct((t, D_MODEL), F32),
        scratch_shapes=[pltpu.VMEM((D_MODEL, 2 * D_FF), BF16), pltpu.VMEM((D_FF, D_MODEL), BF16),
                        pltpu.VMEM((tm, D_MODEL), F32)],
        compiler_params=_params(("arbitrary",), 56),
        name="ffn",
    )(x2, gain1, ada_l, ada_l, ada_l, gain2, w_in, w_out)


def _router_kernel(x_ref, g1_ref, sc_ref, sh_ref, wr_ref, h_ref, comb_ref, rank_ref):
    h = _rms(x_ref[...], g1_ref[...]) * (1.0 + sc_ref[...]) + sh_ref[...]
    h_ref[...] = h.astype(h_ref.dtype)
    logits = jnp.dot(h, wr_ref[...], preferred_element_type=F32, precision=lax.Precision.HIGHEST)
    lane = lax.broadcasted_iota(jnp.int32, logits.shape, 1)
    neg = jnp.float32(-jnp.inf)
    lg = jnp.where(lane < N_EXPERTS, logits, neg)
    m1 = jnp.max(lg, axis=-1, keepdims=True)
    i1 = jnp.min(jnp.where(lg == m1, lane, LANES), axis=-1, keepdims=True)
    lg2 = jnp.where(lane == i1, neg, lg)
    m2 = jnp.max(lg2, axis=-1, keepdims=True)
    i2 = jnp.min(jnp.where(lg2 == m2, lane, LANES), axis=-1, keepdims=True)
    e2 = jnp.exp(m2 - m1)
    p1 = 1.0 / (1.0 + e2)
    p2 = e2 / (1.0 + e2)
    comb = jnp.where(lane == i1, p1, 0.0) + jnp.where(lane == i2, p2, 0.0)
    comb_ref[...] = comb
    sel = jnp.where(comb > 0.0, 1.0, 0.0).astype(BF16)
    r = lax.broadcasted_iota(jnp.int32, (MOE_SUB, MOE_SUB), 0)
    c = lax.broadcasted_iota(jnp.int32, (MOE_SUB, MOE_SUB), 1)
    tri = jnp.where(c <= r, 1.0, 0.0).astype(BF16)
    for r0 in range(0, comb.shape[0], MOE_SUB):
        upto = jnp.dot(tri, sel[r0:r0 + MOE_SUB, :], preferred_element_type=F32)
        rank_ref[r0:r0 + MOE_SUB, :] = jnp.where(comb[r0:r0 + MOE_SUB, :] > 0.0, upto - 1.0, -1.0)


def _router(x2, ada_l, gain1, w_router_pad, tiles_per_batch):
    t = x2.shape[0]
    tm = ROW_TILE
    row = lambda i: (i, 0)
    return pl.pallas_call(
        _router_kernel,
        grid=(t // tm,),
        in_specs=[
            pl.BlockSpec((tm, D_MODEL), row),
            _resident((1, D_MODEL), lambda i: (0, 0)),
            _ada_spec(4, tiles_per_batch),
            _ada_spec(3, tiles_per_batch),
            _resident((D_MODEL, LANES), lambda i: (0, 0)),
        ],
        out_specs=[pl.BlockSpec((tm, D_MODEL), row), pl.BlockSpec((tm, LANES), row),
                   pl.BlockSpec((tm, LANES), row)],
        out_shape=[jax.ShapeDtypeStruct((t, D_MODEL), BF16), jax.ShapeDtypeStruct((t, LANES), F32),
                   jax.ShapeDtypeStruct((t, LANES), F32)],
        compiler_params=_params(("parallel",), 32),
        name="router",
    )(x2, gain1, ada_l, ada_l, w_router_pad)


def _moe_kernel(nch_ref, cnt_ref, off_ref, h_ref, srow_ref, scol_ref, comb_ref, wi_ref, wo_ref, o_ref, xg):
    s = pl.program_id(0)
    e = pl.program_id(1)
    n_sub = MOE_SUPER // MOE_SUB
    group = s * N_EXPERTS + e
    n_parts = nch_ref[group]

    @pl.when(e == 0)
    def _():
        o_ref[...] = jnp.zeros(o_ref.shape, F32)

    @pl.when(n_parts > 0)
    def _():
        last = pl.multiple_of((n_parts - 1) * MOE_PART, MOE_PART)
        xg[pl.ds(last, MOE_PART), :] = jnp.zeros((MOE_PART, D_MODEL), BF16)

    dest_col = lax.broadcasted_iota(jnp.int32, (MOE_PART, MOE_SUB), 0).astype(F32)
    dest_row = lax.broadcasted_iota(jnp.int32, (MOE_SUB, MOE_PART), 1).astype(F32)
    lane = lax.broadcasted_iota(jnp.int32, (MOE_SUB, LANES), 1)

    def gathered(j, part):
        srow = srow_ref[:, j * MOE_SUB:(j + 1) * MOE_SUB]
        onehot = jnp.where(srow - float(part * MOE_PART) == dest_col, 1.0, 0.0).astype(BF16)
        rows = jnp.dot(onehot, h_ref[j * MOE_SUB:(j + 1) * MOE_SUB, :], preferred_element_type=F32)
        return rows.astype(BF16)

    for j in range(n_sub):
        start = pl.multiple_of(off_ref[group * n_sub + j], MOE_ALIGN)
        xg[pl.ds(start, MOE_PART), :] = gathered(j, 0)

    part_row = lax.broadcasted_iota(jnp.int32, (MOE_PART, D_MODEL), 0)
    for j in range(n_sub):
        cnt = cnt_ref[group * n_sub + j]
        for part in range(1, MOE_SUB // MOE_PART):
            @pl.when(cnt > part * MOE_PART)
            def _():
                start = pl.multiple_of(off_ref[group * n_sub + j] + part * MOE_PART, MOE_ALIGN)
                cnt_pad = jnp.bitwise_and(cnt + (MOE_ALIGN - 1), -MOE_ALIGN)
                own = part_row < cnt_pad - part * MOE_PART
                xg[pl.ds(start, MOE_PART), :] = jnp.where(own, gathered(j, part), xg[pl.ds(start, MOE_PART), :])

    def swiglu_rows(r0, n_rows):
        x_e = xg[pl.ds(r0, n_rows), :]
        gate = jnp.dot(x_e, wi_ref[:, :D_FF_EXPERT], preferred_element_type=F32)
        up = jnp.dot(x_e, wi_ref[:, D_FF_EXPERT:], preferred_element_type=F32)
        a = (_silu(gate) * up).astype(BF16)
        xg[pl.ds(r0, n_rows), :] = jnp.dot(a, wo_ref[...], preferred_element_type=F32).astype(BF16)

    def chunk(c, carry):
        swiglu_rows(pl.multiple_of(c * MOE_CHUNK, MOE_CHUNK), MOE_CHUNK)
        return carry

    n_full = lax.shift_right_logical(n_parts, 1)
    lax.fori_loop(0, n_full, chunk, 0)

    @pl.when(n_parts % 2 == 1)
    def _():
        swiglu_rows(pl.multiple_of((n_parts - 1) * MOE_PART, MOE_PART), MOE_PART)

    def scatter(j, part):
        rows = slice(j * MOE_SUB, (j + 1) * MOE_SUB)
        sel = lane == e
        scol = jnp.sum(jnp.where(sel, scol_ref[rows, :], 0.0), axis=-1, keepdims=True)
        prob = jnp.sum(jnp.where(sel, comb_ref[rows, :], 0.0), axis=-1, keepdims=True)
        start = pl.multiple_of(off_ref[group * n_sub + j] + part * MOE_PART, MOE_ALIGN)
        onehot = jnp.where(scol - float(part * MOE_PART) == dest_row, 1.0, 0.0).astype(BF16)
        y = xg[pl.ds(start, MOE_PART), :]
        if part > 0:
            cnt_pad = jnp.bitwise_and(cnt_ref[group * n_sub + j] + (MOE_ALIGN - 1), -MOE_ALIGN)
            y = jnp.where(part_row < cnt_pad - part * MOE_PART, y, jnp.zeros_like(y))
        o_ref[rows, :] += prob * jnp.dot(onehot, y, preferred_element_type=F32)

    for j in range(n_sub):
        scatter(j, 0)
    for j in range(n_sub):
        cnt = cnt_ref[group * n_sub + j]
        for part in range(1, MOE_SUB // MOE_PART):
            pl.when(cnt > part * MOE_PART)(functools.partial(scatter, j, part))


def _moe(h, comb, rank, w_in, w_out, layer):
    t = h.shape[0]
    n_super = t // MOE_SUPER
    n_sub = MOE_SUPER // MOE_SUB
    sel = (comb[:, :N_EXPERTS] > 0.0).reshape(n_super, n_sub, MOE_SUB, N_EXPERTS)
    cnt = sel.astype(jnp.int32).sum(axis=2)
    cnt_pad = (cnt + MOE_ALIGN - 1) // MOE_ALIGN * MOE_ALIGN
    off = jnp.cumsum(cnt_pad, axis=1) - cnt_pad
    total = cnt_pad.sum(axis=1)
    rank_rows = (rank[:, :N_EXPERTS].reshape(n_super, MOE_SUPER, N_EXPERTS).transpose(0, 2, 1)
                 .reshape(n_super, N_EXPERTS, 1, MOE_SUPER))
    n_parts = ((total + MOE_PART - 1) // MOE_PART).reshape(-1).astype(jnp.int32)
    by_group = lambda a: a.transpose(0, 2, 1).reshape(-1).astype(jnp.int32)

    xg_rows = -(-(MOE_SUPER + n_sub * (MOE_ALIGN - 1) + MOE_PART) // MOE_CHUNK) * MOE_CHUNK
    once = lambda shape, index_map: pl.BlockSpec(shape, index_map, pipeline_mode=pl.Buffered(1))
    grid_spec = pltpu.PrefetchScalarGridSpec(
        num_scalar_prefetch=3,
        grid=(n_super, N_EXPERTS),
        in_specs=[
            once((MOE_SUPER, D_MODEL), lambda s, e, *_: (s, 0)),
            pl.BlockSpec((None, None, 1, MOE_SUPER), lambda s, e, *_: (s, e, 0, 0)),
            once((MOE_SUPER, LANES), lambda s, e, *_: (s, 0)),
            once((MOE_SUPER, LANES), lambda s, e, *_: (s, 0)),
            pl.BlockSpec((None, D_MODEL, 2 * D_FF_EXPERT), lambda s, e, *_: (layer * N_EXPERTS + e, 0, 0)),
            pl.BlockSpec((None, D_FF_EXPERT, D_MODEL), lambda s, e, *_: (layer * N_EXPERTS + e, 0, 0)),
        ],
        out_specs=pl.BlockSpec((MOE_SUPER, D_MODEL), lambda s, e, *_: (s, 0)),
        scratch_shapes=[pltpu.VMEM((xg_rows, D_MODEL), BF16)],
    )
    return pl.pallas_call(
        _moe_kernel,
        grid_spec=grid_spec,
        out_shape=jax.ShapeDtypeStruct((t, D_MODEL), F32),
        compiler_params=_params(("arbitrary", "arbitrary"), 58),
        name="moe",
    )(n_parts, by_group(cnt), by_group(off), h, rank_rows, rank, comb, w_in, w_out)


def _resid_kernel(x_ref, f_ref, gf_ref, g2_ref, o_ref):
    o_ref[...] = x_ref[...] + gf_ref[...] * _rms(f_ref[...], g2_ref[...])


def _resid(x2, f, ada_l, gain2, tiles_per_batch):
    t = x2.shape[0]
    tm = ROW_TILE
    row = lambda i: (i, 0)
    return pl.pallas_call(
        _resid_kernel,
        grid=(t // tm,),
        in_specs=[
            pl.BlockSpec((tm, D_MODEL), row),
            pl.BlockSpec((tm, D_MODEL), row),
            _ada_spec(5, tiles_per_batch),
            _resident((1, D_MODEL), lambda i: (0, 0)),
        ],
        out_specs=pl.BlockSpec((tm, D_MODEL), row),
        out_shape=jax.ShapeDtypeStruct((t, D_MODEL), F32),
        compiler_params=_params(("parallel",), 32),
        name="resid",
    )(x2, f, ada_l, gain2)


def kernel(x, c, w_ada, b_ada, pre_mix_gain, post_mix_gain, pre_ffn_gain, post_ffn_gain, w_in, conv_w, conv_b,
           w_rg_a, b_rg_a, w_rg_x, b_rg_x, lru_lambda, attn_sinks, w_branch, w_out, w_ffn_in, w_ffn_out,
           w_router, w_moe_in, w_moe_out):
    batch, seq, d = x.shape
    depth = w_in.shape[0]
    t = batch * seq
    tiles_per_batch = seq // ROW_TILE
    row1 = lambda v: v.reshape(1, -1)

    ada = _ada(c, w_ada, b_ada)
    w_moe_in_bf16 = w_moe_in.astype(BF16).reshape(-1, D_MODEL, 2 * D_FF_EXPERT)
    w_moe_out_bf16 = w_moe_out.astype(BF16).reshape(-1, D_FF_EXPERT, D_MODEL)
    x2 = x.reshape(t, d)
    for l in range(depth):
        ada_l = ada[l]
        w_gate = jnp.concatenate([w_rg_a[l], w_rg_x[l]], axis=-1).astype(BF16)
        rnn_out, q, kv, gates = _inrnn(x2, row1(pre_mix_gain[l]), ada_l, w_in, l, conv_w[l], row1(conv_b[l]), w_gate,
                                       row1(b_rg_a[l]), row1(b_rg_x[l]), row1(lru_lambda[l]), tiles_per_batch)
        x2 = _attnmix(q, kv, attn_sinks[l], rnn_out, gates, x2, ada_l, row1(post_mix_gain[l]), w_branch, w_out, l,
                      tiles_per_batch)
        if l % 2 == 0:
            x2 = _ffn(x2, ada_l, row1(pre_ffn_gain[l]), row1(post_ffn_gain[l]), w_ffn_in, w_ffn_out, l // 2,
                      tiles_per_batch)
        else:
            w_r = jnp.pad(w_router[l // 2], ((0, 0), (0, LANES - N_EXPERTS)))
            h, comb, rank = _router(x2, ada_l, row1(pre_ffn_gain[l]), w_r, tiles_per_batch)
            f = _moe(h, comb, rank, w_moe_in_bf16, w_moe_out_bf16, l // 2)
            x2 = _resid(x2, f, ada_l, row1(post_ffn_gain[l]), tiles_per_batch)
    return x2.reshape(batch, seq, d)
```

```python
import functools
import math

import jax
import jax.numpy as jnp
from jax import lax
from jax.experimental import pallas as pl
from jax.experimental.pallas import tpu as pltpu

D_MODEL = 1024
D_RNN = 1024
N_RNN_BLOCKS = 8
RNN_BLOCK = D_RNN // N_RNN_BLOCKS
CONV_WIDTH = 4
LRU_C = 8.0
N_HEADS = 8
N_KV_HEADS = 2
GROUP = N_HEADS // N_KV_HEADS
HEAD_DIM = 128
WINDOW = 128
D_ATTN = N_HEADS * HEAD_DIM
D_KV = N_KV_HEADS * HEAD_DIM
D_GATES = 2 * D_MODEL
D_IN = 2 * D_RNN + D_ATTN + 2 * D_KV + D_GATES
D_FF = 3 * D_MODEL
N_EXPERTS = 8
D_FF_EXPERT = D_FF // 2
EPS = 1e-6

BF16 = jnp.bfloat16
F32 = jnp.float32

SUBLANES = 8
LANES = 128
MIB = 1024 * 1024

ROW_TILE = 512
PROJ_CHUNK = 512
FF_CHUNK = 512
MOE_SUPER = 2048
MOE_SUB = 256
MOE_PART = 128
MOE_ALIGN = 16
MOE_CHUNK = 256
MOE_TAIL = 64
MASK_VALUE = -1e30
LOG2E = math.log2(math.e)


def _params(semantics, vmem_mib):
    return pltpu.CompilerParams(dimension_semantics=semantics, vmem_limit_bytes=vmem_mib * MIB)


def _resident(shape, index_map):
    return pl.BlockSpec(shape, index_map, pipeline_mode=pl.Buffered(1))


def _rms(x, gain):
    return x * lax.rsqrt(jnp.mean(x * x, axis=-1, keepdims=True) + EPS) * gain


def _silu(x):
    return x * jax.nn.sigmoid(x)


def _ada_kernel(c_ref, w_ref, b_ref, o_ref):
    c = c_ref[...]
    o_ref[...] = jnp.dot(_silu(c), w_ref[...], preferred_element_type=F32,
                         precision=lax.Precision.HIGHEST) + b_ref[...]


def _ada(c, w_ada, b_ada):
    n_layers, d, n6 = w_ada.shape
    b = c.shape[0]
    c_pad = jnp.zeros((SUBLANES, d), F32).at[:b].set(c)
    nb = 1536
    out = pl.pallas_call(
        _ada_kernel,
        grid=(n_layers, n6 // nb),
        in_specs=[
            pl.BlockSpec((SUBLANES, d), lambda l, j: (0, 0)),
            pl.BlockSpec((None, d, nb), lambda l, j: (l, 0, j)),
            pl.BlockSpec((None, 1, nb), lambda l, j: (l, 0, j)),
        ],
        out_specs=pl.BlockSpec((None, SUBLANES, nb), lambda l, j: (l, 0, j)),
        out_shape=jax.ShapeDtypeStruct((n_layers, SUBLANES, n6), F32),
        compiler_params=_params(("arbitrary", "arbitrary"), 32),
        name="ada",
    )(c_pad, w_ada, b_ada.reshape(n_layers, 1, n6))
    return out[:, :b].reshape(n_layers, b, 1, n6)


def _ada_spec(col, tiles_per_batch, lead=0):
    return pl.BlockSpec((None, 1, D_MODEL),
                        lambda i, *_: (jnp.maximum(i - lead, 0) // tiles_per_batch, 0, col))


W_STEPS = 8


def _row_spec(tm, width):
    return pl.BlockSpec((tm, width), lambda i: (jnp.maximum(i - W_STEPS, 0), 0))


def _weight_chunk_spec(layer, rows, cols):
    return pl.BlockSpec((None, rows // W_STEPS, cols), lambda i: (layer, jnp.minimum(i, W_STEPS - 1), 0))


def _keep_weight_chunk(i, w_ref, w_scr):
    rows = w_ref.shape[0]
    w_scr[pl.ds(pl.multiple_of(i * rows, rows), rows), :] = w_ref[...].astype(BF16)


def _segment_permutation(ts):
    seg = ts // SUBLANES
    p = jnp.arange(ts)
    src = (p % SUBLANES) * seg + p // SUBLANES
    perm = (src[:, None] == jnp.arange(ts)[None, :]).astype(BF16)
    return perm, perm.T


def _inrnn_kernel(x_ref, gain_ref, sc_ref, sh_ref, w_ref, perm_ref, unperm_ref, cw_ref, cb_ref, wg_ref, ba_ref,
                  bx_ref, lam_ref, o_ref, q_ref, kv_ref, gt_ref, w_scr, xs, gs, outp, hist, hcar, *, tiles_per_batch):
    step = pl.program_id(0)

    @pl.when(step < W_STEPS)
    def _():
        _keep_weight_chunk(step, w_ref, w_scr)

    @pl.when(step >= W_STEPS)
    def _():
        ts = x_ref.shape[0]
        seg = ts // SUBLANES

        @pl.when((step - W_STEPS) % tiles_per_batch == 0)
        def _():
            hist[...] = jnp.zeros((SUBLANES, D_RNN), F32)
            hcar[...] = jnp.zeros((SUBLANES, D_RNN), F32)

        h_in = (_rms(x_ref[...], gain_ref[...]) * (1.0 + sc_ref[...]) + sh_ref[...]).astype(BF16)
        hp = jnp.dot(perm_ref[...], h_in, preferred_element_type=F32).astype(BF16)
        xs[...] = jnp.dot(hp, w_scr[:, :D_RNN], preferred_element_type=F32)
        gs[...] = jnp.dot(hp, w_scr[:, D_RNN:2 * D_RNN], preferred_element_type=F32)
        proj_chunks = []
        col = 2 * D_RNN
        for ref, width in ((q_ref, D_ATTN), (kv_ref, 2 * D_KV), (gt_ref, D_GATES)):
            proj_chunks += [(ref, c, col + c) for c in range(0, width, PROJ_CHUNK)]
            col += width

        def project(ref, c, wcol):
            ref[:, c:c + PROJ_CHUNK] = jnp.dot(
                h_in, w_scr[:, wcol:wcol + PROJ_CHUNK], preferred_element_type=F32).astype(ref.dtype)

        sub = lax.broadcasted_iota(jnp.int32, (SUBLANES, RNN_BLOCK), 0)
        lam = lam_ref[...]
        sp2 = (-LRU_C * LOG2E) * (jnp.maximum(-lam, 0.0) + jnp.log1p(jnp.exp(-jnp.abs(lam))))
        c = math.sqrt(2.0 / math.pi)
        vrow = lambda arr, v: arr[v * SUBLANES:(v + 1) * SUBLANES, :]

        for n in range(N_RNN_BLOCKS):
            cols = slice(n * RNN_BLOCK, (n + 1) * RNN_BLOCK)
            xp = [xs[v * SUBLANES:(v + 1) * SUBLANES, cols] for v in range(seg)]

            def tail(j):
                rolled = pltpu.roll(xp[seg - j], 1, axis=0)
                return jnp.where(sub == 0, jnp.broadcast_to(hist[j:j + 1, cols], (SUBLANES, RNN_BLOCK)), rolled)

            tails = {j: tail(j) for j in range(1, CONV_WIDTH)}
            for j in range(1, CONV_WIDTH):
                hist[j:j + 1, cols] = xp[seg - j][SUBLANES - 1:SUBLANES, :]

            xc = cb_ref[:, cols] + cw_ref[CONV_WIDTH - 1:CONV_WIDTH, cols] * jnp.concatenate(xp, axis=0)
            for j in range(1, CONV_WIDTH):
                k = CONV_WIDTH - 1 - j
                shifted = jnp.concatenate([tails[j - v] for v in range(j)] + xp[:seg - j], axis=0)
                xc = xc + cw_ref[k:k + 1, cols] * shifted

            g = jnp.dot(xc.astype(BF16), wg_ref[n], preferred_element_type=F32)
            r = jax.nn.sigmoid(g[:, :RNN_BLOCK] + ba_ref[:, cols])
            i = jax.nn.sigmoid(g[:, RNN_BLOCK:] + bx_ref[:, cols])
            a = jnp.exp2(r * sp2[:, cols])
            w = 1.0 - a * a
            u = (w * lax.rsqrt(jnp.maximum(w, 1e-30))) * i * xc

            h = jnp.zeros((SUBLANES, RNN_BLOCK), F32)
            prod = jnp.ones((SUBLANES, RNN_BLOCK), F32)
            hs, prods = [], []
            for v in range(seg):
                h = vrow(a, v) * h + vrow(u, v)
                prod = vrow(a, v) * prod
                hs.append(h)
                prods.append(prod)

            pa, pb = prod, h
            for s in (1, 2, 4):
                a_sh = pltpu.roll(pa, s, axis=0)
                b_sh = pltpu.roll(pb, s, axis=0)
                m = sub >= s
                pb = jnp.where(m, pa * b_sh + pb, pb)
                pa = jnp.where(m, pa * a_sh, pa)
            h_end = pb + pa * hcar[:, cols]
            carry_in = jnp.where(sub == 0, hcar[:, cols], pltpu.roll(h_end, 1, axis=0))
            hcar[:, cols] = jnp.broadcast_to(h_end[SUBLANES - 1:SUBLANES, :], (SUBLANES, RNN_BLOCK))

            h_all = jnp.concatenate([hs[v] + prods[v] * carry_in for v in range(seg)], axis=0)

            gg = gs[:, cols]
            th = jnp.tanh(gg * (c + (c * 0.044715) * (gg * gg)))
            hg = h_all * (0.5 * gg)
            outp[:, cols] = (hg + hg * th).astype(BF16)
            if n < len(proj_chunks):
                project(*proj_chunks[n])

        for chunk in proj_chunks[N_RNN_BLOCKS:]:
            project(*chunk)
        o_ref[...] = jnp.dot(unperm_ref[...], outp[...], preferred_element_type=F32).astype(o_ref.dtype)


def _inrnn(x2, gain, ada_l, w_in, layer, conv_w, conv_b, w_gate, b_a, b_x, lam, tiles_per_batch):
    t = x2.shape[0]
    tm = ROW_TILE
    widths = (D_RNN, D_ATTN, 2 * D_KV, D_GATES)
    perm, unperm = _segment_permutation(tm)
    const2 = lambda i: (0, 0)
    return pl.pallas_call(
        functools.partial(_inrnn_kernel, tiles_per_batch=tiles_per_batch),
        grid=(W_STEPS + t // tm,),
        in_specs=[
            _row_spec(tm, D_MODEL),
            _resident((1, D_MODEL), const2),
            _ada_spec(1, tiles_per_batch, W_STEPS),
            _ada_spec(0, tiles_per_batch, W_STEPS),
            _weight_chunk_spec(layer, D_MODEL, D_IN),
            _resident((tm, tm), const2),
            _resident((tm, tm), const2),
            _resident((CONV_WIDTH, D_RNN), const2),
            _resident((1, D_RNN), const2),
            _resident((N_RNN_BLOCKS, RNN_BLOCK, 2 * RNN_BLOCK), lambda i: (0, 0, 0)),
            _resident((1, D_RNN), const2),
            _resident((1, D_RNN), const2),
            _resident((1, D_RNN), const2),
        ],
        out_specs=[_row_spec(tm, w) for w in widths],
        out_shape=[jax.ShapeDtypeStruct((t, w), BF16) for w in widths],
        scratch_shapes=[
            pltpu.VMEM((D_MODEL, D_IN), BF16),
            pltpu.VMEM((tm, D_RNN), F32),
            pltpu.VMEM((tm, D_RNN), F32),
            pltpu.VMEM((tm, D_RNN), BF16),
            pltpu.VMEM((SUBLANES, D_RNN), F32),
            pltpu.VMEM((SUBLANES, D_RNN), F32),
        ],
        compiler_params=_params(("arbitrary",), 52),
        name="inrnn",
    )(x2, gain, ada_l, ada_l, w_in, perm, unperm, conv_w, conv_b, w_gate, b_a, b_x, lam)


def _attn_bias(sinks):
    qi = jnp.arange(WINDOW)[:, None]
    sj = jnp.arange(2 * WINDOW)[None, :]
    dist = (qi + WINDOW - sj).astype(F32)
    valid = (dist >= 0) & (dist < WINDOW)
    slopes = jnp.asarray([2.0 ** (-8.0 * (h + 1) / N_HEADS) for h in range(N_HEADS)], F32)
    bias = jnp.where(valid[None], -slopes[:, None, None] * dist[None], MASK_VALUE)
    bias = jnp.where((sj == 0)[None], sinks.astype(F32)[:, None, None], bias)
    return (bias * LOG2E).reshape(N_KV_HEADS, GROUP * WINDOW, 2 * WINDOW)


def _router_kernel(x_ref, g1_ref, sc_ref, sh_ref, wr_ref, h_ref, comb_ref, rank_ref):
    h = _rms(x_ref[...], g1_ref[...]) * (1.0 + sc_ref[...]) + sh_ref[...]
    h_hi = h.astype(BF16)
    h_ref[...] = h_hi
    h_lo = (h - h_hi.astype(F32)).astype(BF16)
    w = wr_ref[...]
    w_hi = w.astype(BF16)
    w_lo = (w - w_hi.astype(F32)).astype(BF16)
    dot = functools.partial(jnp.dot, preferred_element_type=F32)
    logits = dot(h_hi, w_hi) + (dot(h_lo, w_hi) + dot(h_hi, w_lo))
    lane = lax.broadcasted_iota(jnp.int32, logits.shape, 1).astype(F32)
    neg = jnp.float32(-jnp.inf)
    lg = jnp.where(lane < N_EXPERTS, logits, neg)
    m1 = jnp.max(lg, axis=-1, keepdims=True)
    i1 = jnp.min(jnp.where(lg == m1, lane, float(LANES)), axis=-1, keepdims=True)
    lg2 = jnp.where(lane == i1, neg, lg)
    m2 = jnp.max(lg2, axis=-1, keepdims=True)
    i2 = jnp.min(jnp.where(lg2 == m2, lane, float(LANES)), axis=-1, keepdims=True)
    e2 = jnp.exp(m2 - m1)
    p1 = 1.0 / (1.0 + e2)
    p2 = e2 / (1.0 + e2)
    comb = jnp.where(lane == i1, p1, 0.0) + jnp.where(lane == i2, p2, 0.0)
    comb_ref[...] = comb
    sel = jnp.where(comb > 0.0, 1.0, 0.0).astype(BF16)
    r = lax.broadcasted_iota(jnp.int32, (MOE_SUB, MOE_SUB), 0)
    c = lax.broadcasted_iota(jnp.int32, (MOE_SUB, MOE_SUB), 1)
    tri = jnp.where(c <= r, 1.0, 0.0).astype(BF16)
    for r0 in range(0, comb.shape[0], MOE_SUB):
        upto = jnp.dot(tri, sel[r0:r0 + MOE_SUB, :], preferred_element_type=F32)
        rank_ref[r0:r0 + MOE_SUB, :] = jnp.where(comb[r0:r0 + MOE_SUB, :] > 0.0, upto - 1.0, -1.0)


def _router(x2, ada_l, gain1, w_router_pad, tiles_per_batch):
    t = x2.shape[0]
    tm = ROW_TILE
    row = lambda i: (i, 0)
    return pl.pallas_call(
        _router_kernel,
        grid=(t // tm,),
        in_specs=[
            pl.BlockSpec((tm, D_MODEL), row),
            _resident((1, D_MODEL), lambda i: (0, 0)),
            _ada_spec(4, tiles_per_batch),
            _ada_spec(3, tiles_per_batch),
            _resident((D_MODEL, LANES), lambda i: (0, 0)),
        ],
        out_specs=[pl.BlockSpec((tm, D_MODEL), row), pl.BlockSpec((tm, LANES), row),
                   pl.BlockSpec((tm, LANES), row)],
        out_shape=[jax.ShapeDtypeStruct((t, D_MODEL), BF16), jax.ShapeDtypeStruct((t, LANES), F32),
                   jax.ShapeDtypeStruct((t, LANES), F32)],
        compiler_params=_params(("parallel",), 32),
        name="router",
    )(x2, gain1, ada_l, ada_l, w_router_pad)


def _attnmix_kernel(q_ref, kv_ref, kvp_ref, bias_ref, rnn_ref, gt_ref, x_ref, gm_ref, gain_ref, wb_ref, wo_ref,
                    o_ref, wb_scr, wo_scr, s_scr, p_scr, att_scr, m_scr, *, tiles_per_batch):
    step = pl.program_id(0)

    @pl.when(step < W_STEPS)
    def _():
        _keep_weight_chunk(step, wb_ref, wb_scr)
        _keep_weight_chunk(step, wo_ref, wo_scr)

    @pl.when(step >= W_STEPS)
    def _():
        tq = q_ref.shape[0]
        first = (step - W_STEPS) % tiles_per_batch == 0
        col = lax.broadcasted_iota(jnp.int32, (GROUP * WINDOW, 2 * WINDOW), 1)
        first_mask = jnp.where(first & (col >= 1) & (col < WINDOW), MASK_VALUE, 0.0).astype(F32)
        slot0 = lax.broadcasted_iota(jnp.int32, (2 * WINDOW, HEAD_DIM), 0) == 0
        zeros = jnp.zeros((2 * WINDOW, HEAD_DIM), BF16)
        ones = jnp.ones((2 * WINDOW, HEAD_DIM), BF16)
        scale2 = (HEAD_DIM ** -0.5) * LOG2E
        units = [(jb, kvh) for jb in range(tq // WINDOW) for kvh in range(N_KV_HEADS)]

        def keys_or_values(jb, cols):
            r0 = jb * WINDOW
            prev = kvp_ref[:, cols] if jb == 0 else kv_ref[r0 - WINDOW:r0, cols]
            kv = jnp.concatenate([prev, kv_ref[r0:r0 + WINDOW, cols]], axis=0)
            return jnp.where(slot0, zeros, kv)

        for u, (jb, kvh) in enumerate(units):
            r0 = jb * WINDOW
            k = keys_or_values(jb, slice(kvh * HEAD_DIM, (kvh + 1) * HEAD_DIM))
            q = jnp.concatenate(
                [q_ref[r0:r0 + WINDOW, (kvh * GROUP + g) * HEAD_DIM:(kvh * GROUP + g + 1) * HEAD_DIM]
                 for g in range(GROUP)], axis=0)
            s = lax.dot_general(q, k, (((1,), (1,)), ((), ())), preferred_element_type=F32)
            s = s * scale2 + bias_ref[kvh]
            if jb == 0:
                s = s + first_mask
            s_scr[u] = s

        n_chunks = len(units) // 2
        width = D_MODEL // n_chunks

        def recurrent_branch(c):
            cols = slice(c * width, (c + 1) * width)
            bp0 = jnp.dot(rnn_ref[...], wb_scr[:D_MODEL, cols], preferred_element_type=F32)
            m_scr[:, cols] = jax.nn.sigmoid(gt_ref[:, cols].astype(F32)) * bp0

        for u, (jb, kvh) in enumerate(units):
            r0 = jb * WINDOW
            v = keys_or_values(jb, slice(D_KV + kvh * HEAD_DIM, D_KV + (kvh + 1) * HEAD_DIM))
            v_ext = jnp.concatenate([v, ones], axis=1)
            for g in range(GROUP):
                rows = slice(g * WINDOW, (g + 1) * WINDOW)
                s = s_scr[u, rows, :]
                m = jnp.max(s, axis=-1, keepdims=True)
                p_scr[u, rows, :] = jnp.exp2(s - m).astype(BF16)
            o_ext = jnp.dot(p_scr[u], v_ext, preferred_element_type=F32)
            o = o_ext[:, :HEAD_DIM] / o_ext[:, HEAD_DIM:]
            for g in range(GROUP):
                hd = kvh * GROUP + g
                att_scr[r0:r0 + WINDOW, hd * HEAD_DIM:(hd + 1) * HEAD_DIM] = (
                    o[g * WINDOW:(g + 1) * WINDOW, :].astype(BF16))
            if u % 2 == 1:
                recurrent_branch(u // 2)

        bp1 = jnp.dot(att_scr[...], wb_scr[D_MODEL:, :], preferred_element_type=F32)
        g1 = jax.nn.sigmoid(gt_ref[:, D_MODEL:].astype(F32))
        merged = (m_scr[...] + g1 * bp1).astype(BF16)
        mix = jnp.dot(merged, wo_scr[...], preferred_element_type=F32)
        o_ref[...] = x_ref[...] + gm_ref[...] * _rms(mix, gain_ref[...])


def _attnmix(q, kv, sinks, rnn_out, gates, x2, ada_l, gain, w_branch, w_out, layer, tiles_per_batch):
    t = x2.shape[0]
    tm = ROW_TILE
    blocks_per_tile = tm // WINDOW
    n_units = blocks_per_tile * N_KV_HEADS

    def prev_block(i):
        r = jnp.maximum(i - W_STEPS, 0)
        return (r * blocks_per_tile - jnp.where(r % tiles_per_batch == 0, 0, 1), 0)

    return pl.pallas_call(
        functools.partial(_attnmix_kernel, tiles_per_batch=tiles_per_batch),
        grid=(W_STEPS + t // tm,),
        in_specs=[
            _row_spec(tm, D_ATTN),
            _row_spec(tm, 2 * D_KV),
            pl.BlockSpec((WINDOW, 2 * D_KV), prev_block),
            _resident((N_KV_HEADS, GROUP * WINDOW, 2 * WINDOW), lambda i: (0, 0, 0)),
            _row_spec(tm, D_MODEL),
            _row_spec(tm, D_GATES),
            _row_spec(tm, D_MODEL),
            _ada_spec(2, tiles_per_batch, W_STEPS),
            _resident((1, D_MODEL), lambda i: (0, 0)),
            _weight_chunk_spec(layer, 2 * D_MODEL, D_MODEL),
            _weight_chunk_spec(layer, D_MODEL, D_MODEL),
        ],
        out_specs=_row_spec(tm, D_MODEL),
        out_shape=jax.ShapeDtypeStruct((t, D_MODEL), F32),
        scratch_shapes=[
            pltpu.VMEM((2 * D_MODEL, D_MODEL), BF16),
            pltpu.VMEM((D_MODEL, D_MODEL), BF16),
            pltpu.VMEM((n_units, GROUP * WINDOW, 2 * WINDOW), F32),
            pltpu.VMEM((n_units, GROUP * WINDOW, 2 * WINDOW), BF16),
            pltpu.VMEM((tm, D_ATTN), BF16),
            pltpu.VMEM((tm, D_MODEL), F32),
        ],
        compiler_params=_params(("arbitrary",), 48),
        name="attnmix",
    )(q, kv, kv, _attn_bias(sinks), rnn_out, gates, x2, ada_l, gain,
      w_branch.reshape(-1, 2 * D_MODEL, D_MODEL), w_out)


def _ffn_kernel(x_ref, g1_ref, sc_ref, sh_ref, gf_ref, g2_ref, wi_ref, wo_ref, o_ref, wi_scr, wo_scr, acc):
    i = pl.program_id(0)

    @pl.when(i < W_STEPS)
    def _():
        _keep_weight_chunk(i, wi_ref, wi_scr)
        _keep_weight_chunk(i, wo_ref, wo_scr)

    @pl.when(i >= W_STEPS)
    def _():
        x = x_ref[...]
        h = (_rms(x, g1_ref[...]) * (1.0 + sc_ref[...]) + sh_ref[...]).astype(BF16)
        for c in range(0, D_FF, FF_CHUNK):
            gate = jnp.dot(h, wi_scr[:, c:c + FF_CHUNK], preferred_element_type=F32)
            up = jnp.dot(h, wi_scr[:, D_FF + c:D_FF + c + FF_CHUNK], preferred_element_type=F32)
            a = (_silu(gate) * up).astype(BF16)
            part = jnp.dot(a, wo_scr[c:c + FF_CHUNK, :], preferred_element_type=F32)
            if c == 0:
                acc[...] = part
            else:
                acc[...] += part
        o_ref[...] = x + gf_ref[...] * _rms(acc[...], g2_ref[...])


def _ffn(x2, ada_l, gain1, gain2, w_in, w_out, layer, tiles_per_batch):
    t = x2.shape[0]
    tm = ROW_TILE
    return pl.pallas_call(
        _ffn_kernel,
        grid=(W_STEPS + t // tm,),
        in_specs=[
            _row_spec(tm, D_MODEL),
            _resident((1, D_MODEL), lambda i: (0, 0)),
            _ada_spec(4, tiles_per_batch, W_STEPS),
            _ada_spec(3, tiles_per_batch, W_STEPS),
            _ada_spec(5, tiles_per_batch, W_STEPS),
            _resident((1, D_MODEL), lambda i: (0, 0)),
            _weight_chunk_spec(layer, D_MODEL, 2 * D_FF),
            _weight_chunk_spec(layer, D_FF, D_MODEL),
        ],
        out_specs=_row_spec(tm, D_MODEL),
        out_shape=jax.ShapeDtypeStruct((t, D_MODEL), F32),
        scratch_shapes=[pltpu.VMEM((D_MODEL, 2 * D_FF), BF16), pltpu.VMEM((D_FF, D_MODEL), BF16),
                        pltpu.VMEM((tm, D_MODEL), F32)],
        compiler_params=_params(("arbitrary",), 56),
        name="ffn",
    )(x2, gain1, ada_l, ada_l, ada_l, gain2, w_in, w_out)


def _moe_kernel(nch_ref, cnt_ref, off_ref, h_ref, srow_ref, scol_ref, comb_ref, wi_ref, wo_ref, o_ref, xg):
    s = pl.program_id(0)
    e = pl.program_id(1)
    n_sub = MOE_SUPER // MOE_SUB
    group = s * N_EXPERTS + e
    n_units = nch_ref[group]

    @pl.when(e == 0)
    def _():
        o_ref[...] = jnp.zeros(o_ref.shape, F32)

    @pl.when(n_units > 0)
    def _():
        last = pl.multiple_of((n_units - 1) * MOE_TAIL, MOE_TAIL)
        xg[pl.ds(last, MOE_TAIL), :] = jnp.zeros((MOE_TAIL, D_MODEL), BF16)

    dest_col = lax.broadcasted_iota(jnp.int32, (MOE_PART, MOE_SUB), 0).astype(F32)
    dest_row = lax.broadcasted_iota(jnp.int32, (MOE_SUB, MOE_PART), 1).astype(F32)
    lane = lax.broadcasted_iota(jnp.int32, (MOE_SUB, LANES), 1)

    def gathered(j, part):
        srow = srow_ref[:, j * MOE_SUB:(j + 1) * MOE_SUB]
        onehot = jnp.where(srow - float(part * MOE_PART) == dest_col, 1.0, 0.0).astype(BF16)
        rows = jnp.dot(onehot, h_ref[j * MOE_SUB:(j + 1) * MOE_SUB, :], preferred_element_type=F32)
        return rows.astype(BF16)

    for j in range(n_sub):
        start = pl.multiple_of(off_ref[group * n_sub + j], MOE_ALIGN)
        xg[pl.ds(start, MOE_PART), :] = gathered(j, 0)

    part_row = lax.broadcasted_iota(jnp.int32, (MOE_PART, D_MODEL), 0)
    for j in range(n_sub):
        cnt = cnt_ref[group * n_sub + j]
        for part in range(1, MOE_SUB // MOE_PART):
            @pl.when(cnt > part * MOE_PART)
            def _():
                start = pl.multiple_of(off_ref[group * n_sub + j] + part * MOE_PART, MOE_ALIGN)
                cnt_pad = jnp.bitwise_and(cnt + (MOE_ALIGN - 1), -MOE_ALIGN)
                own = part_row < cnt_pad - part * MOE_PART
                xg[pl.ds(start, MOE_PART), :] = jnp.where(own, gathered(j, part), xg[pl.ds(start, MOE_PART), :])

    def swiglu_rows(r0, n_rows):
        x_e = xg[pl.ds(r0, n_rows), :]
        gate = jnp.dot(x_e, wi_ref[:, :D_FF_EXPERT], preferred_element_type=F32)
        up = jnp.dot(x_e, wi_ref[:, D_FF_EXPERT:], preferred_element_type=F32)
        a = (_silu(gate) * up).astype(BF16)
        xg[pl.ds(r0, n_rows), :] = jnp.dot(a, wo_ref[...], preferred_element_type=F32).astype(BF16)

    def chunk(c, carry):
        swiglu_rows(pl.multiple_of(c * MOE_CHUNK, MOE_CHUNK), MOE_CHUNK)
        return carry

    units_per_chunk = MOE_CHUNK // MOE_TAIL
    n_full = n_units // units_per_chunk
    lax.fori_loop(0, n_full, chunk, 0)
    done = n_full * units_per_chunk
    piece = units_per_chunk // 2
    while piece >= 1:
        take = (n_units - done) >= piece

        @pl.when(take)
        def _(done=done, piece=piece):
            swiglu_rows(pl.multiple_of(done * MOE_TAIL, MOE_TAIL), piece * MOE_TAIL)

        done = done + jnp.where(take, piece, 0)
        piece //= 2

    def scatter(j, part):
        rows = slice(j * MOE_SUB, (j + 1) * MOE_SUB)
        sel = lane == e
        scol = jnp.sum(jnp.where(sel, scol_ref[rows, :], 0.0), axis=-1, keepdims=True)
        prob = jnp.sum(jnp.where(sel, comb_ref[rows, :], 0.0), axis=-1, keepdims=True)
        start = pl.multiple_of(off_ref[group * n_sub + j] + part * MOE_PART, MOE_ALIGN)
        onehot = jnp.where(scol - float(part * MOE_PART) == dest_row, 1.0, 0.0).astype(BF16)
        y = xg[pl.ds(start, MOE_PART), :]
        if part > 0:
            cnt_pad = jnp.bitwise_and(cnt_ref[group * n_sub + j] + (MOE_ALIGN - 1), -MOE_ALIGN)
            y = jnp.where(part_row < cnt_pad - part * MOE_PART, y, jnp.zeros_like(y))
        o_ref[rows, :] += prob * jnp.dot(onehot, y, preferred_element_type=F32)

    for j in range(n_sub):
        scatter(j, 0)
    for j in range(n_sub):
        cnt = cnt_ref[group * n_sub + j]
        for part in range(1, MOE_SUB // MOE_PART):
            pl.when(cnt > part * MOE_PART)(functools.partial(scatter, j, part))


def _moe(h, comb, rank, w_in, w_out, layer):
    t = h.shape[0]
    n_super = t // MOE_SUPER
    n_sub = MOE_SUPER // MOE_SUB
    sel = (comb[:, :N_EXPERTS] > 0.0).reshape(n_super, n_sub, MOE_SUB, N_EXPERTS)
    cnt = sel.astype(jnp.int32).sum(axis=2)
    cnt_pad = (cnt + MOE_ALIGN - 1) // MOE_ALIGN * MOE_ALIGN
    off = jnp.cumsum(cnt_pad, axis=1) - cnt_pad
    total = cnt_pad.sum(axis=1)
    rank_rows = (rank[:, :N_EXPERTS].reshape(n_super, MOE_SUPER, N_EXPERTS).transpose(0, 2, 1)
                 .reshape(n_super, N_EXPERTS, 1, MOE_SUPER))
    n_units = ((total + MOE_TAIL - 1) // MOE_TAIL).reshape(-1).astype(jnp.int32)
    by_group = lambda a: a.transpose(0, 2, 1).reshape(-1).astype(jnp.int32)

    xg_rows = -(-(MOE_SUPER + n_sub * (MOE_ALIGN - 1) + MOE_PART) // MOE_CHUNK) * MOE_CHUNK
    once = lambda shape, index_map: pl.BlockSpec(shape, index_map, pipeline_mode=pl.Buffered(1))
    grid_spec = pltpu.PrefetchScalarGridSpec(
        num_scalar_prefetch=3,
        grid=(n_super, N_EXPERTS),
        in_specs=[
            once((MOE_SUPER, D_MODEL), lambda s, e, *_: (s, 0)),
            pl.BlockSpec((None, None, 1, MOE_SUPER), lambda s, e, *_: (s, e, 0, 0)),
            once((MOE_SUPER, LANES), lambda s, e, *_: (s, 0)),
            once((MOE_SUPER, LANES), lambda s, e, *_: (s, 0)),
            pl.BlockSpec((None, D_MODEL, 2 * D_FF_EXPERT), lambda s, e, *_: (layer * N_EXPERTS + e, 0, 0)),
            pl.BlockSpec((None, D_FF_EXPERT, D_MODEL), lambda s, e, *_: (layer * N_EXPERTS + e, 0, 0)),
        ],
        out_specs=pl.BlockSpec((MOE_SUPER, D_MODEL), lambda s, e, *_: (s, 0)),
        scratch_shapes=[pltpu.VMEM((xg_rows, D_MODEL), BF16)],
    )
    return pl.pallas_call(
        _moe_kernel,
        grid_spec=grid_spec,
        out_shape=jax.ShapeDtypeStruct((t, D_MODEL), F32),
        compiler_params=_params(("arbitrary", "arbitrary"), 58),
        name="moe",
    )(n_units, by_group(cnt), by_group(off), h, rank_rows, rank, comb, w_in, w_out)


def _resid_kernel(x_ref, f_ref, gf_ref, g2_ref, o_ref):
    o_ref[...] = x_ref[...] + gf_ref[...] * _rms(f_ref[...], g2_ref[...])


def _resid(x2, f, ada_l, gain2, tiles_per_batch):
    t = x2.shape[0]
    tm = ROW_TILE
    row = lambda i: (i, 0)
    return pl.pallas_call(
        _resid_kernel,
        grid=(t // tm,),
        in_specs=[
            pl.BlockSpec((tm, D_MODEL), row),
            pl.BlockSpec((tm, D_MODEL), row),
            _ada_spec(5, tiles_per_batch),
            _resident((1, D_MODEL), lambda i: (0, 0)),
        ],
        out_specs=pl.BlockSpec((tm, D_MODEL), row),
        out_shape=jax.ShapeDtypeStruct((t, D_MODEL), F32),
        compiler_params=_params(("parallel",), 32),
        name="resid",
    )(x2, f, ada_l, gain2)


def kernel(x, c, w_ada, b_ada, pre_mix_gain, post_mix_gain, pre_ffn_gain, post_ffn_gain, w_in, conv_w, conv_b,
           w_rg_a, b_rg_a, w_rg_x, b_rg_x, lru_lambda, attn_sinks, w_branch, w_out, w_ffn_in, w_ffn_out,
           w_router, w_moe_in, w_moe_out):
    batch, seq, d = x.shape
    depth = w_in.shape[0]
    t = batch * seq
    tiles_per_batch = seq // ROW_TILE
    row1 = lambda v: v.reshape(1, -1)

    ada = _ada(c, w_ada, b_ada)
    w_moe_in_bf16 = w_moe_in.astype(BF16).reshape(-1, D_MODEL, 2 * D_FF_EXPERT)
    w_moe_out_bf16 = w_moe_out.astype(BF16).reshape(-1, D_FF_EXPERT, D_MODEL)
    x2 = x.reshape(t, d)
    for l in range(depth):
        ada_l = ada[l]
        w_gate = jnp.concatenate([w_rg_a[l], w_rg_x[l]], axis=-1).astype(BF16)
        rnn_out, q, kv, gates = _inrnn(x2, row1(pre_mix_gain[l]), ada_l, w_in, l, conv_w[l], row1(conv_b[l]), w_gate,
                                       row1(b_rg_a[l]), row1(b_rg_x[l]), row1(lru_lambda[l]), tiles_per_batch)
        x2 = _attnmix(q, kv, attn_sinks[l], rnn_out, gates, x2, ada_l, row1(post_mix_gain[l]), w_branch, w_out, l,
                      tiles_per_batch)
        if l % 2 == 0:
            x2 = _ffn(x2, ada_l, row1(pre_ffn_gain[l]), row1(post_ffn_gain[l]), w_ffn_in, w_ffn_out, l // 2,
                      tiles_per_batch)
        else:
            w_r = jnp.pad(w_router[l // 2], ((0, 0), (0, LANES - N_EXPERTS)))
            h, comb, rank = _router(x2, ada_l, row1(pre_ffn_gain[l]), w_r, tiles_per_batch)
            f = _moe(h, comb, rank, w_moe_in_bf16, w_moe_out_bf16, l // 2)
            x2 = _resid(x2, f, ada_l, row1(post_ffn_gain[l]), tiles_per_batch)
    return x2.reshape(batch, seq, d)
```

```python
import functools
import math

import jax
import jax.numpy as jnp
from jax import lax
from jax.experimental import pallas as pl
from jax.experimental.pallas import tpu as pltpu

D_MODEL = 1024
D_RNN = 1024
N_RNN_BLOCKS = 8
RNN_BLOCK = D_RNN // N_RNN_BLOCKS
CONV_WIDTH = 4
LRU_C = 8.0
N_HEADS = 8
N_KV_HEADS = 2
GROUP = N_HEADS // N_KV_HEADS
HEAD_DIM = 128
WINDOW = 128
D_ATTN = N_HEADS * HEAD_DIM
D_KV = N_KV_HEADS * HEAD_DIM
D_GATES = 2 * D_MODEL
D_IN = 2 * D_RNN + D_ATTN + 2 * D_KV + D_GATES
D_FF = 3 * D_MODEL
N_EXPERTS = 8
D_FF_EXPERT = D_FF // 2
EPS = 1e-6

BF16 = jnp.bfloat16
F32 = jnp.float32

SUBLANES = 8
LANES = 128
MIB = 1024 * 1024

ROW_TILE = 512
PROJ_CHUNK = 512
FF_CHUNK = 512
MOE_SUPER = 2048
MOE_SUB = 256
MOE_PART = 128
MOE_ALIGN = 16
MOE_CHUNK = 256
MOE_TAIL = 128
MASK_VALUE = -1e30
LOG2E = math.log2(math.e)


def _params(semantics, vmem_mib):
    return pltpu.CompilerParams(dimension_semantics=semantics, vmem_limit_bytes=vmem_mib * MIB)


def _resident(shape, index_map):
    return pl.BlockSpec(shape, index_map, pipeline_mode=pl.Buffered(1))


def _rms(x, gain):
    return x * lax.rsqrt(jnp.mean(x * x, axis=-1, keepdims=True) + EPS) * gain


def _silu(x):
    return x * jax.nn.sigmoid(x)


def _ada_kernel(c_ref, w_ref, b_ref, o_ref):
    c = c_ref[...]
    o_ref[...] = jnp.dot(_silu(c), w_ref[...], preferred_element_type=F32,
                         precision=lax.Precision.HIGHEST) + b_ref[...]


def _ada(c, w_ada, b_ada):
    n_layers, d, n6 = w_ada.shape
    b = c.shape[0]
    c_pad = jnp.zeros((SUBLANES, d), F32).at[:b].set(c)
    nb = 1536
    out = pl.pallas_call(
        _ada_kernel,
        grid=(n_layers, n6 // nb),
        in_specs=[
            pl.BlockSpec((SUBLANES, d), lambda l, j: (0, 0)),
            pl.BlockSpec((None, d, nb), lambda l, j: (l, 0, j)),
            pl.BlockSpec((None, 1, nb), lambda l, j: (l, 0, j)),
        ],
        out_specs=pl.BlockSpec((None, SUBLANES, nb), lambda l, j: (l, 0, j)),
        out_shape=jax.ShapeDtypeStruct((n_layers, SUBLANES, n6), F32),
        compiler_params=_params(("arbitrary", "arbitrary"), 32),
        name="ada",
    )(c_pad, w_ada, b_ada.reshape(n_layers, 1, n6))
    return out[:, :b].reshape(n_layers, b, 1, n6)


def _ada_spec(col, tiles_per_batch, lead=0):
    return pl.BlockSpec((None, 1, D_MODEL),
                        lambda i, *_: (jnp.maximum(i - lead, 0) // tiles_per_batch, 0, col))


W_STEPS = 8


def _row_spec(tm, width):
    return pl.BlockSpec((tm, width), lambda i: (jnp.maximum(i - W_STEPS, 0), 0))


def _weight_chunk_spec(layer, rows, cols):
    return pl.BlockSpec((None, rows // W_STEPS, cols), lambda i: (layer, jnp.minimum(i, W_STEPS - 1), 0))


def _keep_weight_chunk(i, w_ref, w_scr):
    rows = w_ref.shape[0]
    w_scr[pl.ds(pl.multiple_of(i * rows, rows), rows), :] = w_ref[...].astype(BF16)


def _segment_permutation(ts):
    seg = ts // SUBLANES
    p = jnp.arange(ts)
    src = (p % SUBLANES) * seg + p // SUBLANES
    perm = (src[:, None] == jnp.arange(ts)[None, :]).astype(BF16)
    return perm, perm.T


def _inrnn_kernel(x_ref, gain_ref, sc_ref, sh_ref, w_ref, perm_ref, unperm_ref, cw_ref, cb_ref, wg_ref, ba_ref,
                  bx_ref, lam_ref, o_ref, q_ref, kv_ref, gt_ref, w_scr, xs, gs, outp, hist, hcar, *, tiles_per_batch):
    step = pl.program_id(0)

    @pl.when(step < W_STEPS)
    def _():
        _keep_weight_chunk(step, w_ref, w_scr)

    @pl.when(step >= W_STEPS)
    def _():
        ts = x_ref.shape[0]
        seg = ts // SUBLANES

        @pl.when((step - W_STEPS) % tiles_per_batch == 0)
        def _():
            hist[...] = jnp.zeros((SUBLANES, D_RNN), F32)
            hcar[...] = jnp.zeros((SUBLANES, D_RNN), F32)

        h_in = (_rms(x_ref[...], gain_ref[...]) * (1.0 + sc_ref[...]) + sh_ref[...]).astype(BF16)
        hp = jnp.dot(perm_ref[...], h_in, preferred_element_type=F32).astype(BF16)
        xs[...] = jnp.dot(hp, w_scr[:, :D_RNN], preferred_element_type=F32)
        gs[...] = jnp.dot(hp, w_scr[:, D_RNN:2 * D_RNN], preferred_element_type=F32)
        proj_chunks = []
        col = 2 * D_RNN
        for ref, width in ((q_ref, D_ATTN), (kv_ref, 2 * D_KV), (gt_ref, D_GATES)):
            proj_chunks += [(ref, c, col + c) for c in range(0, width, PROJ_CHUNK)]
            col += width

        def project(ref, c, wcol):
            ref[:, c:c + PROJ_CHUNK] = jnp.dot(
                h_in, w_scr[:, wcol:wcol + PROJ_CHUNK], preferred_element_type=F32).astype(ref.dtype)

        sub = lax.broadcasted_iota(jnp.int32, (SUBLANES, RNN_BLOCK), 0)
        lam = lam_ref[...]
        sp2 = (-LRU_C * LOG2E) * (jnp.maximum(-lam, 0.0) + jnp.log1p(jnp.exp(-jnp.abs(lam))))
        c = math.sqrt(2.0 / math.pi)
        vrow = lambda arr, v: arr[v * SUBLANES:(v + 1) * SUBLANES, :]

        for n in range(N_RNN_BLOCKS):
            cols = slice(n * RNN_BLOCK, (n + 1) * RNN_BLOCK)
            xp = [xs[v * SUBLANES:(v + 1) * SUBLANES, cols] for v in range(seg)]

            def tail(j):
                rolled = pltpu.roll(xp[seg - j], 1, axis=0)
                return jnp.where(sub == 0, jnp.broadcast_to(hist[j:j + 1, cols], (SUBLANES, RNN_BLOCK)), rolled)

            tails = {j: tail(j) for j in range(1, CONV_WIDTH)}
            for j in range(1, CONV_WIDTH):
                hist[j:j + 1, cols] = xp[seg - j][SUBLANES - 1:SUBLANES, :]

            xc = cb_ref[:, cols] + cw_ref[CONV_WIDTH - 1:CONV_WIDTH, cols] * jnp.concatenate(xp, axis=0)
            for j in range(1, CONV_WIDTH):
                k = CONV_WIDTH - 1 - j
                shifted = jnp.concatenate([tails[j - v] for v in range(j)] + xp[:seg - j], axis=0)
                xc = xc + cw_ref[k:k + 1, cols] * shifted

            g = jnp.dot(xc.astype(BF16), wg_ref[n], preferred_element_type=F32)
            r = jax.nn.sigmoid(g[:, :RNN_BLOCK] + ba_ref[:, cols])
            i = jax.nn.sigmoid(g[:, RNN_BLOCK:] + bx_ref[:, cols])
            a = jnp.exp2(r * sp2[:, cols])
            w = 1.0 - a * a
            u = (w * lax.rsqrt(jnp.maximum(w, 1e-30))) * i * xc

            h = jnp.zeros((SUBLANES, RNN_BLOCK), F32)
            prod = jnp.ones((SUBLANES, RNN_BLOCK), F32)
            hs, prods = [], []
            for v in range(seg):
                h = vrow(a, v) * h + vrow(u, v)
                prod = vrow(a, v) * prod
                hs.append(h)
                prods.append(prod)

            pa, pb = prod, h
            for s in (1, 2, 4):
                a_sh = pltpu.roll(pa, s, axis=0)
                b_sh = pltpu.roll(pb, s, axis=0)
                m = sub >= s
                pb = jnp.where(m, pa * b_sh + pb, pb)
                pa = jnp.where(m, pa * a_sh, pa)
            h_end = pb + pa * hcar[:, cols]
            carry_in = jnp.where(sub == 0, hcar[:, cols], pltpu.roll(h_end, 1, axis=0))
            hcar[:, cols] = jnp.broadcast_to(h_end[SUBLANES - 1:SUBLANES, :], (SUBLANES, RNN_BLOCK))

            h_all = jnp.concatenate([hs[v] + prods[v] * carry_in for v in range(seg)], axis=0)

            gg = gs[:, cols]
            th = jnp.tanh(gg * (c + (c * 0.044715) * (gg * gg)))
            hg = h_all * (0.5 * gg)
            outp[:, cols] = (hg + hg * th).astype(BF16)
            if n < len(proj_chunks):
                project(*proj_chunks[n])

        for chunk in proj_chunks[N_RNN_BLOCKS:]:
            project(*chunk)
        o_ref[...] = jnp.dot(unperm_ref[...], outp[...], preferred_element_type=F32).astype(o_ref.dtype)


def _inrnn(x2, gain, ada_l, w_in, layer, conv_w, conv_b, w_gate, b_a, b_x, lam, tiles_per_batch):
    t = x2.shape[0]
    tm = ROW_TILE
    widths = (D_RNN, D_ATTN, 2 * D_KV, D_GATES)
    perm, unperm = _segment_permutation(tm)
    const2 = lambda i: (0, 0)
    return pl.pallas_call(
        functools.partial(_inrnn_kernel, tiles_per_batch=tiles_per_batch),
        grid=(W_STEPS + t // tm,),
        in_specs=[
            _row_spec(tm, D_MODEL),
            _resident((1, D_MODEL), const2),
            _ada_spec(1, tiles_per_batch, W_STEPS),
            _ada_spec(0, tiles_per_batch, W_STEPS),
            _weight_chunk_spec(layer, D_MODEL, D_IN),
            _resident((tm, tm), const2),
            _resident((tm, tm), const2),
            _resident((CONV_WIDTH, D_RNN), const2),
            _resident((1, D_RNN), const2),
            _resident((N_RNN_BLOCKS, RNN_BLOCK, 2 * RNN_BLOCK), lambda i: (0, 0, 0)),
            _resident((1, D_RNN), const2),
            _resident((1, D_RNN), const2),
            _resident((1, D_RNN), const2),
        ],
        out_specs=[_row_spec(tm, w) for w in widths],
        out_shape=[jax.ShapeDtypeStruct((t, w), BF16) for w in widths],
        scratch_shapes=[
            pltpu.VMEM((D_MODEL, D_IN), BF16),
            pltpu.VMEM((tm, D_RNN), F32),
            pltpu.VMEM((tm, D_RNN), F32),
            pltpu.VMEM((tm, D_RNN), BF16),
            pltpu.VMEM((SUBLANES, D_RNN), F32),
            pltpu.VMEM((SUBLANES, D_RNN), F32),
        ],
        compiler_params=_params(("arbitrary",), 52),
        name="inrnn",
    )(x2, gain, ada_l, ada_l, w_in, perm, unperm, conv_w, conv_b, w_gate, b_a, b_x, lam)


def _attn_bias(sinks):
    qi = jnp.arange(WINDOW)[:, None]
    sj = jnp.arange(2 * WINDOW)[None, :]
    dist = (qi + WINDOW - sj).astype(F32)
    valid = (dist >= 0) & (dist < WINDOW)
    slopes = jnp.asarray([2.0 ** (-8.0 * (h + 1) / N_HEADS) for h in range(N_HEADS)], F32)
    bias = jnp.where(valid[None], -slopes[:, None, None] * dist[None], MASK_VALUE)
    bias = jnp.where((sj == 0)[None], sinks.astype(F32)[:, None, None], bias)
    return (bias * LOG2E).reshape(N_KV_HEADS, GROUP * WINDOW, 2 * WINDOW)


def _router_kernel(x_ref, g1_ref, sc_ref, sh_ref, wr_ref, h_ref, comb_ref, rank_ref):
    h = _rms(x_ref[...], g1_ref[...]) * (1.0 + sc_ref[...]) + sh_ref[...]
    h_hi = h.astype(BF16)
    h_ref[...] = h_hi
    h_lo = (h - h_hi.astype(F32)).astype(BF16)
    w = wr_ref[...]
    w_hi = w.astype(BF16)
    w_lo = (w - w_hi.astype(F32)).astype(BF16)
    dot = functools.partial(jnp.dot, preferred_element_type=F32)
    logits = dot(h_hi, w_hi) + (dot(h_lo, w_hi) + dot(h_hi, w_lo))
    lane = lax.broadcasted_iota(jnp.int32, logits.shape, 1).astype(F32)
    neg = jnp.float32(-jnp.inf)
    lg = jnp.where(lane < N_EXPERTS, logits, neg)
    m1 = jnp.max(lg, axis=-1, keepdims=True)
    i1 = jnp.min(jnp.where(lg == m1, lane, float(LANES)), axis=-1, keepdims=True)
    lg2 = jnp.where(lane == i1, neg, lg)
    m2 = jnp.max(lg2, axis=-1, keepdims=True)
    i2 = jnp.min(jnp.where(lg2 == m2, lane, float(LANES)), axis=-1, keepdims=True)
    e2 = jnp.exp(m2 - m1)
    p1 = 1.0 / (1.0 + e2)
    p2 = e2 / (1.0 + e2)
    comb = jnp.where(lane == i1, p1, 0.0) + jnp.where(lane == i2, p2, 0.0)
    comb_ref[...] = comb
    sel = jnp.where(comb > 0.0, 1.0, 0.0).astype(BF16)
    r = lax.broadcasted_iota(jnp.int32, (MOE_SUB, MOE_SUB), 0)
    c = lax.broadcasted_iota(jnp.int32, (MOE_SUB, MOE_SUB), 1)
    tri = jnp.where(c <= r, 1.0, 0.0).astype(BF16)
    for r0 in range(0, comb.shape[0], MOE_SUB):
        upto = jnp.dot(tri, sel[r0:r0 + MOE_SUB, :], preferred_element_type=F32)
        rank_ref[r0:r0 + MOE_SUB, :] = jnp.where(comb[r0:r0 + MOE_SUB, :] > 0.0, upto - 1.0, -1.0)


def _router(x2, ada_l, gain1, w_router_pad, tiles_per_batch):
    t = x2.shape[0]
    tm = ROW_TILE
    row = lambda i: (i, 0)
    return pl.pallas_call(
        _router_kernel,
        grid=(t // tm,),
        in_specs=[
            pl.BlockSpec((tm, D_MODEL), row),
            _resident((1, D_MODEL), lambda i: (0, 0)),
            _ada_spec(4, tiles_per_batch),
            _ada_spec(3, tiles_per_batch),
            _resident((D_MODEL, LANES), lambda i: (0, 0)),
        ],
        out_specs=[pl.BlockSpec((tm, D_MODEL), row), pl.BlockSpec((tm, LANES), row),
                   pl.BlockSpec((tm, LANES), row)],
        out_shape=[jax.ShapeDtypeStruct((t, D_MODEL), BF16), jax.ShapeDtypeStruct((t, LANES), F32),
                   jax.ShapeDtypeStruct((t, LANES), F32)],
        compiler_params=_params(("parallel",), 32),
        name="router",
    )(x2, gain1, ada_l, ada_l, w_router_pad)


def _attnmix_kernel(q_ref, kv_ref, kvp_ref, bias_ref, rnn_ref, gt_ref, x_ref, gm_ref, gain_ref, wb_ref, wo_ref,
                    cast_ref, o_ref, cast_out_ref, wb_scr, wo_scr, s_scr, p_scr, att_scr, m_scr, *,
                    tiles_per_batch):
    step = pl.program_id(0)

    @pl.when(step < W_STEPS)
    def _():
        _keep_weight_chunk(step, wb_ref, wb_scr)
        _keep_weight_chunk(step, wo_ref, wo_scr)

    @pl.when(step >= W_STEPS)
    def _():
        tq = q_ref.shape[0]
        first = (step - W_STEPS) % tiles_per_batch == 0
        col = lax.broadcasted_iota(jnp.int32, (GROUP * WINDOW, 2 * WINDOW), 1)
        first_mask = jnp.where(first & (col >= 1) & (col < WINDOW), MASK_VALUE, 0.0).astype(F32)
        slot0 = lax.broadcasted_iota(jnp.int32, (2 * WINDOW, HEAD_DIM), 0) == 0
        zeros = jnp.zeros((2 * WINDOW, HEAD_DIM), BF16)
        ones = jnp.ones((2 * WINDOW, HEAD_DIM), BF16)
        scale2 = (HEAD_DIM ** -0.5) * LOG2E
        units = [(jb, kvh) for jb in range(tq // WINDOW) for kvh in range(N_KV_HEADS)]

        def keys_or_values(jb, cols):
            r0 = jb * WINDOW
            prev = kvp_ref[:, cols] if jb == 0 else kv_ref[r0 - WINDOW:r0, cols]
            kv = jnp.concatenate([prev, kv_ref[r0:r0 + WINDOW, cols]], axis=0)
            return jnp.where(slot0, zeros, kv)

        for u, (jb, kvh) in enumerate(units):
            r0 = jb * WINDOW
            k = keys_or_values(jb, slice(kvh * HEAD_DIM, (kvh + 1) * HEAD_DIM))
            q = jnp.concatenate(
                [q_ref[r0:r0 + WINDOW, (kvh * GROUP + g) * HEAD_DIM:(kvh * GROUP + g + 1) * HEAD_DIM]
                 for g in range(GROUP)], axis=0)
            s = lax.dot_general(q, k, (((1,), (1,)), ((), ())), preferred_element_type=F32)
            s = s * scale2 + bias_ref[kvh]
            if jb == 0:
                s = s + first_mask
            s_scr[u] = s

        n_chunks = len(units) // 2
        width = D_MODEL // n_chunks

        def recurrent_branch(c):
            cols = slice(c * width, (c + 1) * width)
            bp0 = jnp.dot(rnn_ref[...], wb_scr[:D_MODEL, cols], preferred_element_type=F32)
            m_scr[:, cols] = jax.nn.sigmoid(gt_ref[:, cols].astype(F32)) * bp0

        for u, (jb, kvh) in enumerate(units):
            r0 = jb * WINDOW
            v = keys_or_values(jb, slice(D_KV + kvh * HEAD_DIM, D_KV + (kvh + 1) * HEAD_DIM))
            v_ext = jnp.concatenate([v, ones], axis=1)
            for g in range(GROUP):
                rows = slice(g * WINDOW, (g + 1) * WINDOW)
                s = s_scr[u, rows, :]
                m = jnp.max(s, axis=-1, keepdims=True)
                p_scr[u, rows, :] = jnp.exp2(s - m).astype(BF16)
            o_ext = jnp.dot(p_scr[u], v_ext, preferred_element_type=F32)
            o = o_ext[:, :HEAD_DIM] / o_ext[:, HEAD_DIM:]
            for g in range(GROUP):
                hd = kvh * GROUP + g
                att_scr[r0:r0 + WINDOW, hd * HEAD_DIM:(hd + 1) * HEAD_DIM] = (
                    o[g * WINDOW:(g + 1) * WINDOW, :].astype(BF16))
            if u % 2 == 1:
                recurrent_branch(u // 2)

        bp1 = jnp.dot(att_scr[...], wb_scr[D_MODEL:, :], preferred_element_type=F32)
        g1 = jax.nn.sigmoid(gt_ref[:, D_MODEL:].astype(F32))
        merged = (m_scr[...] + g1 * bp1).astype(BF16)
        mix = jnp.dot(merged, wo_scr[...], preferred_element_type=F32)
        o_ref[...] = x_ref[...] + gm_ref[...] * _rms(mix, gain_ref[...])
        cast_out_ref[...] = cast_ref[...].astype(BF16)


def _attnmix(q, kv, sinks, rnn_out, gates, x2, ada_l, gain, w_branch, w_out, layer, tiles_per_batch,
             expert_w, expert_layer):
    t = x2.shape[0]
    tm = ROW_TILE
    n_tiles = t // tm
    blocks_per_tile = tm // WINDOW
    n_units = blocks_per_tile * N_KV_HEADS
    _, n_exp, w_rows, w_cols = expert_w.shape
    cast_rows = n_exp * w_rows // n_tiles
    cast_block = lambda i: (expert_layer * n_tiles + jnp.maximum(i - W_STEPS, 0), 0)

    def prev_block(i):
        r = jnp.maximum(i - W_STEPS, 0)
        return (r * blocks_per_tile - jnp.where(r % tiles_per_batch == 0, 0, 1), 0)

    x_new, w_bf16 = pl.pallas_call(
        functools.partial(_attnmix_kernel, tiles_per_batch=tiles_per_batch),
        grid=(W_STEPS + n_tiles,),
        in_specs=[
            _row_spec(tm, D_ATTN),
            _row_spec(tm, 2 * D_KV),
            pl.BlockSpec((WINDOW, 2 * D_KV), prev_block),
            _resident((N_KV_HEADS, GROUP * WINDOW, 2 * WINDOW), lambda i: (0, 0, 0)),
            _row_spec(tm, D_MODEL),
            _row_spec(tm, D_GATES),
            _row_spec(tm, D_MODEL),
            _ada_spec(2, tiles_per_batch, W_STEPS),
            _resident((1, D_MODEL), lambda i: (0, 0)),
            _weight_chunk_spec(layer, 2 * D_MODEL, D_MODEL),
            _weight_chunk_spec(layer, D_MODEL, D_MODEL),
            pl.BlockSpec((cast_rows, w_cols), cast_block),
        ],
        out_specs=[_row_spec(tm, D_MODEL), _row_spec(cast_rows, w_cols)],
        out_shape=[jax.ShapeDtypeStruct((t, D_MODEL), F32),
                   jax.ShapeDtypeStruct((n_exp * w_rows, w_cols), BF16)],
        scratch_shapes=[
            pltpu.VMEM((2 * D_MODEL, D_MODEL), BF16),
            pltpu.VMEM((D_MODEL, D_MODEL), BF16),
            pltpu.VMEM((n_units, GROUP * WINDOW, 2 * WINDOW), F32),
            pltpu.VMEM((n_units, GROUP * WINDOW, 2 * WINDOW), BF16),
            pltpu.VMEM((tm, D_ATTN), BF16),
            pltpu.VMEM((tm, D_MODEL), F32),
        ],
        compiler_params=_params(("arbitrary",), 56),
        name="attnmix",
    )(q, kv, kv, _attn_bias(sinks), rnn_out, gates, x2, ada_l, gain,
      w_branch.reshape(-1, 2 * D_MODEL, D_MODEL), w_out, expert_w.reshape(-1, w_cols))
    return x_new, w_bf16.reshape(n_exp, w_rows, w_cols)


def _ffn_kernel(x_ref, g1_ref, sc_ref, sh_ref, gf_ref, g2_ref, wi_ref, wo_ref, o_ref, wi_scr, wo_scr, acc):
    i = pl.program_id(0)

    @pl.when(i < W_STEPS)
    def _():
        _keep_weight_chunk(i, wi_ref, wi_scr)
        _keep_weight_chunk(i, wo_ref, wo_scr)

    @pl.when(i >= W_STEPS)
    def _():
        x = x_ref[...]
        h = (_rms(x, g1_ref[...]) * (1.0 + sc_ref[...]) + sh_ref[...]).astype(BF16)
        for c in range(0, D_FF, FF_CHUNK):
            gate = jnp.dot(h, wi_scr[:, c:c + FF_CHUNK], preferred_element_type=F32)
            up = jnp.dot(h, wi_scr[:, D_FF + c:D_FF + c + FF_CHUNK], preferred_element_type=F32)
            a = (_silu(gate) * up).astype(BF16)
            part = jnp.dot(a, wo_scr[c:c + FF_CHUNK, :], preferred_element_type=F32)
            if c == 0:
                acc[...] = part
            else:
                acc[...] += part
        o_ref[...] = x + gf_ref[...] * _rms(acc[...], g2_ref[...])


def _ffn(x2, ada_l, gain1, gain2, w_in, w_out, layer, tiles_per_batch):
    t = x2.shape[0]
    tm = ROW_TILE
    return pl.pallas_call(
        _ffn_kernel,
        grid=(W_STEPS + t // tm,),
        in_specs=[
            _row_spec(tm, D_MODEL),
            _resident((1, D_MODEL), lambda i: (0, 0)),
            _ada_spec(4, tiles_per_batch, W_STEPS),
            _ada_spec(3, tiles_per_batch, W_STEPS),
            _ada_spec(5, tiles_per_batch, W_STEPS),
            _resident((1, D_MODEL), lambda i: (0, 0)),
            _weight_chunk_spec(layer, D_MODEL, 2 * D_FF),
            _weight_chunk_spec(layer, D_FF, D_MODEL),
        ],
        out_specs=_row_spec(tm, D_MODEL),
        out_shape=jax.ShapeDtypeStruct((t, D_MODEL), F32),
        scratch_shapes=[pltpu.VMEM((D_MODEL, 2 * D_FF), BF16), pltpu.VMEM((D_FF, D_MODEL), BF16),
                        pltpu.VMEM((tm, D_MODEL), F32)],
        compiler_params=_params(("arbitrary",), 56),
        name="ffn",
    )(x2, gain1, ada_l, ada_l, ada_l, gain2, w_in, w_out)


def _moe_kernel(nch_ref, cnt_ref, off_ref, h_ref, srow_ref, scol_ref, comb_ref, wi_ref, wo_ref, o_ref, xg):
    s = pl.program_id(0)
    e = pl.program_id(1)
    n_sub = MOE_SUPER // MOE_SUB
    group = s * N_EXPERTS + e
    n_units = nch_ref[group]

    @pl.when(e == 0)
    def _():
        o_ref[...] = jnp.zeros(o_ref.shape, F32)

    @pl.when(n_units > 0)
    def _():
        last = pl.multiple_of((n_units - 1) * MOE_TAIL, MOE_TAIL)
        xg[pl.ds(last, MOE_TAIL), :] = jnp.zeros((MOE_TAIL, D_MODEL), BF16)

    dest_col = lax.broadcasted_iota(jnp.int32, (MOE_PART, MOE_SUB), 0).astype(F32)
    dest_row = lax.broadcasted_iota(jnp.int32, (MOE_SUB, MOE_PART), 1).astype(F32)
    lane = lax.broadcasted_iota(jnp.int32, (MOE_SUB, LANES), 1)

    def gathered(j, part):
        srow = srow_ref[:, j * MOE_SUB:(j + 1) * MOE_SUB]
        onehot = jnp.where(srow - float(part * MOE_PART) == dest_col, 1.0, 0.0).astype(BF16)
        rows = jnp.dot(onehot, h_ref[j * MOE_SUB:(j + 1) * MOE_SUB, :], preferred_element_type=F32)
        return rows.astype(BF16)

    for j in range(n_sub):
        start = pl.multiple_of(off_ref[group * n_sub + j], MOE_ALIGN)
        xg[pl.ds(start, MOE_PART), :] = gathered(j, 0)

    part_row = lax.broadcasted_iota(jnp.int32, (MOE_PART, D_MODEL), 0)
    for j in range(n_sub):
        cnt = cnt_ref[group * n_sub + j]
        for part in range(1, MOE_SUB // MOE_PART):
            @pl.when(cnt > part * MOE_PART)
            def _():
                start = pl.multiple_of(off_ref[group * n_sub + j] + part * MOE_PART, MOE_ALIGN)
                cnt_pad = jnp.bitwise_and(cnt + (MOE_ALIGN - 1), -MOE_ALIGN)
                own = part_row < cnt_pad - part * MOE_PART
                xg[pl.ds(start, MOE_PART), :] = jnp.where(own, gathered(j, part), xg[pl.ds(start, MOE_PART), :])

    def swiglu_rows(r0, n_rows):
        x_e = xg[pl.ds(r0, n_rows), :]
        gate = jnp.dot(x_e, wi_ref[:, :D_FF_EXPERT], preferred_element_type=F32)
        up = jnp.dot(x_e, wi_ref[:, D_FF_EXPERT:], preferred_element_type=F32)
        a = (_silu(gate) * up).astype(BF16)
        xg[pl.ds(r0, n_rows), :] = jnp.dot(a, wo_ref[...], preferred_element_type=F32).astype(BF16)

    def chunk(c, carry):
        swiglu_rows(pl.multiple_of(c * MOE_CHUNK, MOE_CHUNK), MOE_CHUNK)
        return carry

    units_per_chunk = MOE_CHUNK // MOE_TAIL
    n_full = n_units // units_per_chunk
    lax.fori_loop(0, n_full, chunk, 0)
    done = n_full * units_per_chunk
    piece = units_per_chunk // 2
    while piece >= 1:
        take = (n_units - done) >= piece

        @pl.when(take)
        def _(done=done, piece=piece):
            swiglu_rows(pl.multiple_of(done * MOE_TAIL, MOE_TAIL), piece * MOE_TAIL)

        done = done + jnp.where(take, piece, 0)
        piece //= 2

    def scatter(j, part):
        rows = slice(j * MOE_SUB, (j + 1) * MOE_SUB)
        sel = lane == e
        scol = jnp.sum(jnp.where(sel, scol_ref[rows, :], 0.0), axis=-1, keepdims=True)
        prob = jnp.sum(jnp.where(sel, comb_ref[rows, :], 0.0), axis=-1, keepdims=True)
        start = pl.multiple_of(off_ref[group * n_sub + j] + part * MOE_PART, MOE_ALIGN)
        onehot = jnp.where(scol - float(part * MOE_PART) == dest_row, 1.0, 0.0).astype(BF16)
        y = xg[pl.ds(start, MOE_PART), :]
        if part > 0:
            cnt_pad = jnp.bitwise_and(cnt_ref[group * n_sub + j] + (MOE_ALIGN - 1), -MOE_ALIGN)
            y = jnp.where(part_row < cnt_pad - part * MOE_PART, y, jnp.zeros_like(y))
        o_ref[rows, :] += prob * jnp.dot(onehot, y, preferred_element_type=F32)

    for j in range(n_sub):
        scatter(j, 0)
    for j in range(n_sub):
        cnt = cnt_ref[group * n_sub + j]
        for part in range(1, MOE_SUB // MOE_PART):
            pl.when(cnt > part * MOE_PART)(functools.partial(scatter, j, part))


def _moe(h, comb, rank, w_in, w_out, layer):
    t = h.shape[0]
    n_super = t // MOE_SUPER
    n_sub = MOE_SUPER // MOE_SUB
    sel = (comb[:, :N_EXPERTS] > 0.0).reshape(n_super, n_sub, MOE_SUB, N_EXPERTS)
    cnt = sel.astype(jnp.int32).sum(axis=2)
    cnt_pad = (cnt + MOE_ALIGN - 1) // MOE_ALIGN * MOE_ALIGN
    off = jnp.cumsum(cnt_pad, axis=1) - cnt_pad
    total = cnt_pad.sum(axis=1)
    rank_rows = (rank[:, :N_EXPERTS].reshape(n_super, MOE_SUPER, N_EXPERTS).transpose(0, 2, 1)
                 .reshape(n_super, N_EXPERTS, 1, MOE_SUPER))
    n_units = ((total + MOE_TAIL - 1) // MOE_TAIL).reshape(-1).astype(jnp.int32)
    by_group = lambda a: a.transpose(0, 2, 1).reshape(-1).astype(jnp.int32)

    xg_rows = -(-(MOE_SUPER + n_sub * (MOE_ALIGN - 1) + MOE_PART) // MOE_CHUNK) * MOE_CHUNK
    once = lambda shape, index_map: pl.BlockSpec(shape, index_map, pipeline_mode=pl.Buffered(1))
    grid_spec = pltpu.PrefetchScalarGridSpec(
        num_scalar_prefetch=3,
        grid=(n_super, N_EXPERTS),
        in_specs=[
            once((MOE_SUPER, D_MODEL), lambda s, e, *_: (s, 0)),
            pl.BlockSpec((None, None, 1, MOE_SUPER), lambda s, e, *_: (s, e, 0, 0)),
            once((MOE_SUPER, LANES), lambda s, e, *_: (s, 0)),
            once((MOE_SUPER, LANES), lambda s, e, *_: (s, 0)),
            pl.BlockSpec((None, D_MODEL, 2 * D_FF_EXPERT), lambda s, e, *_: (layer * N_EXPERTS + e, 0, 0)),
            pl.BlockSpec((None, D_FF_EXPERT, D_MODEL), lambda s, e, *_: (layer * N_EXPERTS + e, 0, 0)),
        ],
        out_specs=pl.BlockSpec((MOE_SUPER, D_MODEL), lambda s, e, *_: (s, 0)),
        scratch_shapes=[pltpu.VMEM((xg_rows, D_MODEL), BF16)],
    )
    return pl.pallas_call(
        _moe_kernel,
        grid_spec=grid_spec,
        out_shape=jax.ShapeDtypeStruct((t, D_MODEL), F32),
        compiler_params=_params(("arbitrary", "arbitrary"), 58),
        name="moe",
    )(n_units, by_group(cnt), by_group(off), h, rank_rows, rank, comb, w_in, w_out)


def _resid_kernel(x_ref, f_ref, gf_ref, g2_ref, o_ref):
    o_ref[...] = x_ref[...] + gf_ref[...] * _rms(f_ref[...], g2_ref[...])


def _resid(x2, f, ada_l, gain2, tiles_per_batch):
    t = x2.shape[0]
    tm = ROW_TILE
    row = lambda i: (i, 0)
    return pl.pallas_call(
        _resid_kernel,
        grid=(t // tm,),
        in_specs=[
            pl.BlockSpec((tm, D_MODEL), row),
            pl.BlockSpec((tm, D_MODEL), row),
            _ada_spec(5, tiles_per_batch),
            _resident((1, D_MODEL), lambda i: (0, 0)),
        ],
        out_specs=pl.BlockSpec((tm, D_MODEL), row),
        out_shape=jax.ShapeDtypeStruct((t, D_MODEL), F32),
        compiler_params=_params(("parallel",), 32),
        name="resid",
    )(x2, f, ada_l, gain2)


def kernel(x, c, w_ada, b_ada, pre_mix_gain, post_mix_gain, pre_ffn_gain, post_ffn_gain, w_in, conv_w, conv_b,
           w_rg_a, b_rg_a, w_rg_x, b_rg_x, lru_lambda, attn_sinks, w_branch, w_out, w_ffn_in, w_ffn_out,
           w_router, w_moe_in, w_moe_out):
    batch, seq, d = x.shape
    depth = w_in.shape[0]
    t = batch * seq
    tiles_per_batch = seq // ROW_TILE
    row1 = lambda v: v.reshape(1, -1)

    ada = _ada(c, w_ada, b_ada)
    n_routed = w_moe_in.shape[0]
    x2 = x.reshape(t, d)
    for l in range(depth):
        ada_l = ada[l]
        w_gate = jnp.concatenate([w_rg_a[l], w_rg_x[l]], axis=-1).astype(BF16)
        rnn_out, q, kv, gates = _inrnn(x2, row1(pre_mix_gain[l]), ada_l, w_in, l, conv_w[l], row1(conv_b[l]), w_gate,
                                       row1(b_rg_a[l]), row1(b_rg_x[l]), row1(lru_lambda[l]), tiles_per_batch)
        routed = min(l // 2, n_routed - 1)
        x2, expert_w = _attnmix(q, kv, attn_sinks[l], rnn_out, gates, x2, ada_l, row1(post_mix_gain[l]), w_branch,
                                w_out, l, tiles_per_batch, w_moe_in if l % 2 == 0 else w_moe_out, routed)
        if l % 2 == 0:
            w_moe_in_bf16 = expert_w
            x2 = _ffn(x2, ada_l, row1(pre_ffn_gain[l]), row1(post_ffn_gain[l]), w_ffn_in, w_ffn_out, l // 2,
                      tiles_per_batch)
        else:
            w_r = jnp.pad(w_router[l // 2], ((0, 0), (0, LANES - N_EXPERTS)))
            h, comb, rank = _router(x2, ada_l, row1(pre_ffn_gain[l]), w_r, tiles_per_batch)
            f = _moe(h, comb, rank, w_moe_in_bf16, expert_w, 0)
            x2 = _resid(x2, f, ada_l, row1(post_ffn_gain[l]), tiles_per_batch)
    return x2.reshape(batch, seq, d)
```

```python
import functools
import math

import jax
import jax.numpy as jnp
from jax import lax
from jax.experimental import pallas as pl
from jax.experimental.pallas import tpu as pltpu

D_MODEL = 1024
D_RNN = 1024
N_RNN_BLOCKS = 8
RNN_BLOCK = D_RNN // N_RNN_BLOCKS
CONV_WIDTH = 4
LRU_C = 8.0
N_HEADS = 8
N_KV_HEADS = 2
GROUP = N_HEADS // N_KV_HEADS
HEAD_DIM = 128
WINDOW = 128
D_ATTN = N_HEADS * HEAD_DIM
D_KV = N_KV_HEADS * HEAD_DIM
D_GATES = 2 * D_MODEL
D_IN = 2 * D_RNN + D_ATTN + 2 * D_KV + D_GATES
D_FF = 3 * D_MODEL
N_EXPERTS = 8
D_FF_EXPERT = D_FF // 2
EPS = 1e-6

BF16 = jnp.bfloat16
F32 = jnp.float32

SUBLANES = 8
LANES = 128
MIB = 1024 * 1024

ROW_TILE = 512
PROJ_CHUNK = 512
FF_CHUNK = 512
MOE_SUPER = 2048
MOE_SUB = 256
MOE_PART = 128
MOE_ALIGN = 16
MOE_CHUNK = 256
MOE_TAIL = 128
MASK_VALUE = -1e30
LOG2E = math.log2(math.e)


def _params(semantics, vmem_mib):
    return pltpu.CompilerParams(dimension_semantics=semantics, vmem_limit_bytes=vmem_mib * MIB)


def _resident(shape, index_map):
    return pl.BlockSpec(shape, index_map, pipeline_mode=pl.Buffered(1))


def _rms(x, gain):
    return x * lax.rsqrt(jnp.mean(x * x, axis=-1, keepdims=True) + EPS) * gain


def _silu(x):
    return x * jax.nn.sigmoid(x)


def _ada_kernel(c_ref, w_ref, b_ref, o_ref):
    w = w_ref[...]
    d, nb = w.shape
    for b in range(c_ref.shape[0]):
        col = _silu(c_ref[b])
        prod = w * jnp.concatenate([col] * (nb // LANES), axis=1)
        part = jnp.sum(prod.reshape(d // SUBLANES, SUBLANES, nb), axis=0)
        o_ref[b] = jnp.sum(part, axis=0, keepdims=True) + b_ref[...]


def _ada(c, w_ada, b_ada):
    n_layers, d, n6 = w_ada.shape
    b = c.shape[0]
    c_cols = jnp.broadcast_to(c[:, :, None], (b, d, LANES))
    nb = 1536
    return pl.pallas_call(
        _ada_kernel,
        grid=(n_layers, n6 // nb),
        in_specs=[
            pl.BlockSpec((b, d, LANES), lambda l, j: (0, 0, 0)),
            pl.BlockSpec((None, d, nb), lambda l, j: (l, 0, j)),
            pl.BlockSpec((None, 1, nb), lambda l, j: (l, 0, j)),
        ],
        out_specs=pl.BlockSpec((None, b, 1, nb), lambda l, j: (l, 0, 0, j)),
        out_shape=jax.ShapeDtypeStruct((n_layers, b, 1, n6), F32),
        compiler_params=_params(("arbitrary", "arbitrary"), 40),
        name="ada",
    )(c_cols, w_ada, b_ada.reshape(n_layers, 1, n6))


def _ada_spec(col, tiles_per_batch, lead=0):
    return pl.BlockSpec((None, 1, D_MODEL),
                        lambda i, *_: (jnp.maximum(i - lead, 0) // tiles_per_batch, 0, col))


W_STEPS = 8


def _row_spec(tm, width):
    return pl.BlockSpec((tm, width), lambda i: (jnp.maximum(i - W_STEPS, 0), 0))


def _weight_chunk_spec(layer, rows, cols):
    return pl.BlockSpec((None, rows // W_STEPS, cols), lambda i: (layer, jnp.minimum(i, W_STEPS - 1), 0))


def _keep_weight_chunk(i, w_ref, w_scr):
    rows = w_ref.shape[0]
    w_scr[pl.ds(pl.multiple_of(i * rows, rows), rows), :] = w_ref[...].astype(BF16)


def _segment_permutation(ts):
    seg = ts // SUBLANES
    p = jnp.arange(ts)
    src = (p % SUBLANES) * seg + p // SUBLANES
    perm = (src[:, None] == jnp.arange(ts)[None, :]).astype(BF16)
    return perm, perm.T


def _inrnn_kernel(x_ref, gain_ref, sc_ref, sh_ref, w_ref, perm_ref, unperm_ref, cw_ref, cb_ref, wg_ref, ba_ref,
                  bx_ref, lam_ref, o_ref, q_ref, kv_ref, gt_ref, w_scr, xs, gs, outp, hist, hcar, *, tiles_per_batch):
    step = pl.program_id(0)

    @pl.when(step < W_STEPS)
    def _():
        _keep_weight_chunk(step, w_ref, w_scr)

    @pl.when(step >= W_STEPS)
    def _():
        ts = x_ref.shape[0]
        seg = ts // SUBLANES

        @pl.when((step - W_STEPS) % tiles_per_batch == 0)
        def _():
            hist[...] = jnp.zeros((SUBLANES, D_RNN), F32)
            hcar[...] = jnp.zeros((SUBLANES, D_RNN), F32)

        h_in = (_rms(x_ref[...], gain_ref[...]) * (1.0 + sc_ref[...]) + sh_ref[...]).astype(BF16)
        hp = jnp.dot(perm_ref[...], h_in, preferred_element_type=F32).astype(BF16)
        xs[...] = jnp.dot(hp, w_scr[:, :D_RNN], preferred_element_type=F32)
        gs[...] = jnp.dot(hp, w_scr[:, D_RNN:2 * D_RNN], preferred_element_type=F32)
        proj_chunks = []
        col = 2 * D_RNN
        for ref, width in ((q_ref, D_ATTN), (kv_ref, 2 * D_KV), (gt_ref, D_GATES)):
            proj_chunks += [(ref, c, col + c) for c in range(0, width, PROJ_CHUNK)]
            col += width

        def project(ref, c, wcol):
            ref[:, c:c + PROJ_CHUNK] = jnp.dot(
                h_in, w_scr[:, wcol:wcol + PROJ_CHUNK], preferred_element_type=F32).astype(ref.dtype)

        sub = lax.broadcasted_iota(jnp.int32, (SUBLANES, RNN_BLOCK), 0)
        lam = lam_ref[...]
        sp2 = (-LRU_C * LOG2E) * (jnp.maximum(-lam, 0.0) + jnp.log1p(jnp.exp(-jnp.abs(lam))))
        c = math.sqrt(2.0 / math.pi)
        vrow = lambda arr, v: arr[v * SUBLANES:(v + 1) * SUBLANES, :]

        for n in range(N_RNN_BLOCKS):
            cols = slice(n * RNN_BLOCK, (n + 1) * RNN_BLOCK)
            xp = [xs[v * SUBLANES:(v + 1) * SUBLANES, cols] for v in range(seg)]

            def tail(j):
                rolled = pltpu.roll(xp[seg - j], 1, axis=0)
                return jnp.where(sub == 0, jnp.broadcast_to(hist[j:j + 1, cols], (SUBLANES, RNN_BLOCK)), rolled)

            tails = {j: tail(j) for j in range(1, CONV_WIDTH)}
            for j in range(1, CONV_WIDTH):
                hist[j:j + 1, cols] = xp[seg - j][SUBLANES - 1:SUBLANES, :]

            xc = cb_ref[:, cols] + cw_ref[CONV_WIDTH - 1:CONV_WIDTH, cols] * jnp.concatenate(xp, axis=0)
            for j in range(1, CONV_WIDTH):
                k = CONV_WIDTH - 1 - j
                shifted = jnp.concatenate([tails[j - v] for v in range(j)] + xp[:seg - j], axis=0)
                xc = xc + cw_ref[k:k + 1, cols] * shifted

            g = jnp.dot(xc.astype(BF16), wg_ref[n], preferred_element_type=F32)
            r = jax.nn.sigmoid(g[:, :RNN_BLOCK] + ba_ref[:, cols])
            i = jax.nn.sigmoid(g[:, RNN_BLOCK:] + bx_ref[:, cols])
            a = jnp.exp2(r * sp2[:, cols])
            w = 1.0 - a * a
            u = (w * lax.rsqrt(jnp.maximum(w, 1e-30))) * i * xc

            h = jnp.zeros((SUBLANES, RNN_BLOCK), F32)
            prod = jnp.ones((SUBLANES, RNN_BLOCK), F32)
            hs, prods = [], []
            for v in range(seg):
                h = vrow(a, v) * h + vrow(u, v)
                prod = vrow(a, v) * prod
                hs.append(h)
                prods.append(prod)

            pa, pb = prod, h
            for s in (1, 2, 4):
                a_sh = pltpu.roll(pa, s, axis=0)
                b_sh = pltpu.roll(pb, s, axis=0)
                m = sub >= s
                pb = jnp.where(m, pa * b_sh + pb, pb)
                pa = jnp.where(m, pa * a_sh, pa)
            h_end = pb + pa * hcar[:, cols]
            carry_in = jnp.where(sub == 0, hcar[:, cols], pltpu.roll(h_end, 1, axis=0))
            hcar[:, cols] = jnp.broadcast_to(h_end[SUBLANES - 1:SUBLANES, :], (SUBLANES, RNN_BLOCK))

            h_all = jnp.concatenate([hs[v] + prods[v] * carry_in for v in range(seg)], axis=0)

            gg = gs[:, cols]
            th = jnp.tanh(gg * (c + (c * 0.044715) * (gg * gg)))
            hg = h_all * (0.5 * gg)
            outp[:, cols] = (hg + hg * th).astype(BF16)
            if n < len(proj_chunks):
                project(*proj_chunks[n])

        for chunk in proj_chunks[N_RNN_BLOCKS:]:
            project(*chunk)
        o_ref[...] = jnp.dot(unperm_ref[...], outp[...], preferred_element_type=F32).astype(o_ref.dtype)


def _inrnn(x2, gain, ada_l, w_in, layer, conv_w, conv_b, w_gate, b_a, b_x, lam, tiles_per_batch):
    t = x2.shape[0]
    tm = ROW_TILE
    widths = (D_RNN, D_ATTN, 2 * D_KV, D_GATES)
    perm, unperm = _segment_permutation(tm)
    const2 = lambda i: (0, 0)
    return pl.pallas_call(
        functools.partial(_inrnn_kernel, tiles_per_batch=tiles_per_batch),
        grid=(W_STEPS + t // tm,),
        in_specs=[
            _row_spec(tm, D_MODEL),
            _resident((1, D_MODEL), const2),
            _ada_spec(1, tiles_per_batch, W_STEPS),
            _ada_spec(0, tiles_per_batch, W_STEPS),
            _weight_chunk_spec(layer, D_MODEL, D_IN),
            _resident((tm, tm), const2),
            _resident((tm, tm), const2),
            _resident((CONV_WIDTH, D_RNN), const2),
            _resident((1, D_RNN), const2),
            _resident((N_RNN_BLOCKS, RNN_BLOCK, 2 * RNN_BLOCK), lambda i: (0, 0, 0)),
            _resident((1, D_RNN), const2),
            _resident((1, D_RNN), const2),
            _resident((1, D_RNN), const2),
        ],
        out_specs=[_row_spec(tm, w) for w in widths],
        out_shape=[jax.ShapeDtypeStruct((t, w), BF16) for w in widths],
        scratch_shapes=[
            pltpu.VMEM((D_MODEL, D_IN), BF16),
            pltpu.VMEM((tm, D_RNN), F32),
            pltpu.VMEM((tm, D_RNN), F32),
            pltpu.VMEM((tm, D_RNN), BF16),
            pltpu.VMEM((SUBLANES, D_RNN), F32),
            pltpu.VMEM((SUBLANES, D_RNN), F32),
        ],
        compiler_params=_params(("arbitrary",), 52),
        name="inrnn",
    )(x2, gain, ada_l, ada_l, w_in, perm, unperm, conv_w, conv_b, w_gate, b_a, b_x, lam)


def _attn_bias(sinks):
    qi = jnp.arange(WINDOW)[:, None]
    sj = jnp.arange(2 * WINDOW)[None, :]
    dist = (qi + WINDOW - sj).astype(F32)
    valid = (dist >= 0) & (dist < WINDOW)
    slopes = jnp.asarray([2.0 ** (-8.0 * (h + 1) / N_HEADS) for h in range(N_HEADS)], F32)
    bias = jnp.where(valid[None], -slopes[:, None, None] * dist[None], MASK_VALUE)
    bias = jnp.where((sj == 0)[None], sinks.astype(F32)[:, None, None], bias)
    return (bias * LOG2E).reshape(N_KV_HEADS, GROUP * WINDOW, 2 * WINDOW)


def _router_kernel(x_ref, g1_ref, sc_ref, sh_ref, wr_ref, h_ref, comb_ref, rank_ref):
    h = _rms(x_ref[...], g1_ref[...]) * (1.0 + sc_ref[...]) + sh_ref[...]
    h_hi = h.astype(BF16)
    h_ref[...] = h_hi
    h_lo = (h - h_hi.astype(F32)).astype(BF16)
    w = wr_ref[...]
    w_hi = w.astype(BF16)
    w_lo = (w - w_hi.astype(F32)).astype(BF16)
    dot = functools.partial(jnp.dot, preferred_element_type=F32)
    logits = dot(h_hi, w_hi) + (dot(h_lo, w_hi) + dot(h_hi, w_lo))
    lane = lax.broadcasted_iota(jnp.int32, logits.shape, 1).astype(F32)
    neg = jnp.float32(-jnp.inf)
    lg = jnp.where(lane < N_EXPERTS, logits, neg)
    m1 = jnp.max(lg, axis=-1, keepdims=True)
    i1 = jnp.min(jnp.where(lg == m1, lane, float(LANES)), axis=-1, keepdims=True)
    lg2 = jnp.where(lane == i1, neg, lg)
    m2 = jnp.max(lg2, axis=-1, keepdims=True)
    i2 = jnp.min(jnp.where(lg2 == m2, lane, float(LANES)), axis=-1, keepdims=True)
    e2 = jnp.exp(m2 - m1)
    p1 = 1.0 / (1.0 + e2)
    p2 = e2 / (1.0 + e2)
    comb = jnp.where(lane == i1, p1, 0.0) + jnp.where(lane == i2, p2, 0.0)
    comb_ref[...] = comb
    sel = jnp.where(comb > 0.0, 1.0, 0.0).astype(BF16)
    r = lax.broadcasted_iota(jnp.int32, (MOE_SUB, MOE_SUB), 0)
    c = lax.broadcasted_iota(jnp.int32, (MOE_SUB, MOE_SUB), 1)
    tri = jnp.where(c <= r, 1.0, 0.0).astype(BF16)
    for r0 in range(0, comb.shape[0], MOE_SUB):
        upto = jnp.dot(tri, sel[r0:r0 + MOE_SUB, :], preferred_element_type=F32)
        rank_ref[r0:r0 + MOE_SUB, :] = jnp.where(comb[r0:r0 + MOE_SUB, :] > 0.0, upto - 1.0, -1.0)


def _router(x2, ada_l, gain1, w_router_pad, tiles_per_batch):
    t = x2.shape[0]
    tm = ROW_TILE
    row = lambda i: (i, 0)
    return pl.pallas_call(
        _router_kernel,
        grid=(t // tm,),
        in_specs=[
            pl.BlockSpec((tm, D_MODEL), row),
            _resident((1, D_MODEL), lambda i: (0, 0)),
            _ada_spec(4, tiles_per_batch),
            _ada_spec(3, tiles_per_batch),
            _resident((D_MODEL, LANES), lambda i: (0, 0)),
        ],
        out_specs=[pl.BlockSpec((tm, D_MODEL), row), pl.BlockSpec((tm, LANES), row),
                   pl.BlockSpec((tm, LANES), row)],
        out_shape=[jax.ShapeDtypeStruct((t, D_MODEL), BF16), jax.ShapeDtypeStruct((t, LANES), F32),
                   jax.ShapeDtypeStruct((t, LANES), F32)],
        compiler_params=_params(("parallel",), 32),
        name="router",
    )(x2, gain1, ada_l, ada_l, w_router_pad)


def _attnmix_kernel(q_ref, kv_ref, kvp_ref, bias_ref, rnn_ref, gt_ref, x_ref, gm_ref, gain_ref, wb_ref, wo_ref,
                    cast_ref, o_ref, cast_out_ref, wb_scr, wo_scr, s_scr, p_scr, att_scr, m_scr, *,
                    tiles_per_batch):
    step = pl.program_id(0)

    @pl.when(step < W_STEPS)
    def _():
        _keep_weight_chunk(step, wb_ref, wb_scr)
        _keep_weight_chunk(step, wo_ref, wo_scr)

    @pl.when(step >= W_STEPS)
    def _():
        tq = q_ref.shape[0]
        first = (step - W_STEPS) % tiles_per_batch == 0
        col = lax.broadcasted_iota(jnp.int32, (GROUP * WINDOW, 2 * WINDOW), 1)
        first_mask = jnp.where(first & (col >= 1) & (col < WINDOW), MASK_VALUE, 0.0).astype(F32)
        slot0 = lax.broadcasted_iota(jnp.int32, (2 * WINDOW, HEAD_DIM), 0) == 0
        zeros = jnp.zeros((2 * WINDOW, HEAD_DIM), BF16)
        ones = jnp.ones((2 * WINDOW, HEAD_DIM), BF16)
        scale2 = (HEAD_DIM ** -0.5) * LOG2E
        units = [(jb, kvh) for jb in range(tq // WINDOW) for kvh in range(N_KV_HEADS)]

        def keys_or_values(jb, cols):
            r0 = jb * WINDOW
            prev = kvp_ref[:, cols] if jb == 0 else kv_ref[r0 - WINDOW:r0, cols]
            kv = jnp.concatenate([prev, kv_ref[r0:r0 + WINDOW, cols]], axis=0)
            return jnp.where(slot0, zeros, kv)

        for u, (jb, kvh) in enumerate(units):
            r0 = jb * WINDOW
            k = keys_or_values(jb, slice(kvh * HEAD_DIM, (kvh + 1) * HEAD_DIM))
            q = jnp.concatenate(
                [q_ref[r0:r0 + WINDOW, (kvh * GROUP + g) * HEAD_DIM:(kvh * GROUP + g + 1) * HEAD_DIM]
                 for g in range(GROUP)], axis=0)
            s = lax.dot_general(q, k, (((1,), (1,)), ((), ())), preferred_element_type=F32)
            s = s * scale2 + bias_ref[kvh]
            if jb == 0:
                s = s + first_mask
            s_scr[u] = s

        n_chunks = len(units) // 2
        width = D_MODEL // n_chunks

        def recurrent_branch(c):
            cols = slice(c * width, (c + 1) * width)
            bp0 = jnp.dot(rnn_ref[...], wb_scr[:D_MODEL, cols], preferred_element_type=F32)
            m_scr[:, cols] = jax.nn.sigmoid(gt_ref[:, cols].astype(F32)) * bp0

        for u, (jb, kvh) in enumerate(units):
            r0 = jb * WINDOW
            v = keys_or_values(jb, slice(D_KV + kvh * HEAD_DIM, D_KV + (kvh + 1) * HEAD_DIM))
            v_ext = jnp.concatenate([v, ones], axis=1)
            for g in range(GROUP):
                rows = slice(g * WINDOW, (g + 1) * WINDOW)
                s = s_scr[u, rows, :]
                m = jnp.max(s, axis=-1, keepdims=True)
                p_scr[u, rows, :] = jnp.exp2(s - m).astype(BF16)
            o_ext = jnp.dot(p_scr[u], v_ext, preferred_element_type=F32)
            o = o_ext[:, :HEAD_DIM] / o_ext[:, HEAD_DIM:]
            for g in range(GROUP):
                hd = kvh * GROUP + g
                att_scr[r0:r0 + WINDOW, hd * HEAD_DIM:(hd + 1) * HEAD_DIM] = (
                    o[g * WINDOW:(g + 1) * WINDOW, :].astype(BF16))
            if u % 2 == 1:
                recurrent_branch(u // 2)

        bp1 = jnp.dot(att_scr[...], wb_scr[D_MODEL:, :], preferred_element_type=F32)
        g1 = jax.nn.sigmoid(gt_ref[:, D_MODEL:].astype(F32))
        merged = (m_scr[...] + g1 * bp1).astype(BF16)
        mix = jnp.dot(merged, wo_scr[...], preferred_element_type=F32)
        o_ref[...] = x_ref[...] + gm_ref[...] * _rms(mix, gain_ref[...])
        cast_out_ref[...] = cast_ref[...].astype(BF16)


def _attnmix(q, kv, sinks, rnn_out, gates, x2, ada_l, gain, w_branch, w_out, layer, tiles_per_batch,
             expert_w, expert_layer):
    t = x2.shape[0]
    tm = ROW_TILE
    n_tiles = t // tm
    blocks_per_tile = tm // WINDOW
    n_units = blocks_per_tile * N_KV_HEADS
    _, n_exp, w_rows, w_cols = expert_w.shape
    cast_rows = n_exp * w_rows // n_tiles
    cast_block = lambda i: (expert_layer * n_tiles + jnp.maximum(i - W_STEPS, 0), 0)

    def prev_block(i):
        r = jnp.maximum(i - W_STEPS, 0)
        return (r * blocks_per_tile - jnp.where(r % tiles_per_batch == 0, 0, 1), 0)

    x_new, w_bf16 = pl.pallas_call(
        functools.partial(_attnmix_kernel, tiles_per_batch=tiles_per_batch),
        grid=(W_STEPS + n_tiles,),
        in_specs=[
            _row_spec(tm, D_ATTN),
            _row_spec(tm, 2 * D_KV),
            pl.BlockSpec((WINDOW, 2 * D_KV), prev_block),
            _resident((N_KV_HEADS, GROUP * WINDOW, 2 * WINDOW), lambda i: (0, 0, 0)),
            _row_spec(tm, D_MODEL),
            _row_spec(tm, D_GATES),
            _row_spec(tm, D_MODEL),
            _ada_spec(2, tiles_per_batch, W_STEPS),
            _resident((1, D_MODEL), lambda i: (0, 0)),
            _weight_chunk_spec(layer, 2 * D_MODEL, D_MODEL),
            _weight_chunk_spec(layer, D_MODEL, D_MODEL),
            pl.BlockSpec((cast_rows, w_cols), cast_block),
        ],
        out_specs=[_row_spec(tm, D_MODEL), _row_spec(cast_rows, w_cols)],
        out_shape=[jax.ShapeDtypeStruct((t, D_MODEL), F32),
                   jax.ShapeDtypeStruct((n_exp * w_rows, w_cols), BF16)],
        scratch_shapes=[
            pltpu.VMEM((2 * D_MODEL, D_MODEL), BF16),
            pltpu.VMEM((D_MODEL, D_MODEL), BF16),
            pltpu.VMEM((n_units, GROUP * WINDOW, 2 * WINDOW), F32),
            pltpu.VMEM((n_units, GROUP * WINDOW, 2 * WINDOW), BF16),
            pltpu.VMEM((tm, D_ATTN), BF16),
            pltpu.VMEM((tm, D_MODEL), F32),
        ],
        compiler_params=_params(("arbitrary",), 56),
        name="attnmix",
    )(q, kv, kv, _attn_bias(sinks), rnn_out, gates, x2, ada_l, gain,
      w_branch.reshape(-1, 2 * D_MODEL, D_MODEL), w_out, expert_w.reshape(-1, w_cols))
    return x_new, w_bf16.reshape(n_exp, w_rows, w_cols)


def _ffn_kernel(x_ref, g1_ref, sc_ref, sh_ref, gf_ref, g2_ref, wi_ref, wo_ref, o_ref, wi_scr, wo_scr, acc):
    i = pl.program_id(0)

    @pl.when(i < W_STEPS)
    def _():
        _keep_weight_chunk(i, wi_ref, wi_scr)
        _keep_weight_chunk(i, wo_ref, wo_scr)

    @pl.when(i >= W_STEPS)
    def _():
        x = x_ref[...]
        h = (_rms(x, g1_ref[...]) * (1.0 + sc_ref[...]) + sh_ref[...]).astype(BF16)
        for c in range(0, D_FF, FF_CHUNK):
            gate = jnp.dot(h, wi_scr[:, c:c + FF_CHUNK], preferred_element_type=F32)
            up = jnp.dot(h, wi_scr[:, D_FF + c:D_FF + c + FF_CHUNK], preferred_element_type=F32)
            a = (_silu(gate) * up).astype(BF16)
            part = jnp.dot(a, wo_scr[c:c + FF_CHUNK, :], preferred_element_type=F32)
            if c == 0:
                acc[...] = part
            else:
                acc[...] += part
        o_ref[...] = x + gf_ref[...] * _rms(acc[...], g2_ref[...])


def _ffn(x2, ada_l, gain1, gain2, w_in, w_out, layer, tiles_per_batch):
    t = x2.shape[0]
    tm = ROW_TILE
    return pl.pallas_call(
        _ffn_kernel,
        grid=(W_STEPS + t // tm,),
        in_specs=[
            _row_spec(tm, D_MODEL),
            _resident((1, D_MODEL), lambda i: (0, 0)),
            _ada_spec(4, tiles_per_batch, W_STEPS),
            _ada_spec(3, tiles_per_batch, W_STEPS),
            _ada_spec(5, tiles_per_batch, W_STEPS),
            _resident((1, D_MODEL), lambda i: (0, 0)),
            _weight_chunk_spec(layer, D_MODEL, 2 * D_FF),
            _weight_chunk_spec(layer, D_FF, D_MODEL),
        ],
        out_specs=_row_spec(tm, D_MODEL),
        out_shape=jax.ShapeDtypeStruct((t, D_MODEL), F32),
        scratch_shapes=[pltpu.VMEM((D_MODEL, 2 * D_FF), BF16), pltpu.VMEM((D_FF, D_MODEL), BF16),
                        pltpu.VMEM((tm, D_MODEL), F32)],
        compiler_params=_params(("arbitrary",), 56),
        name="ffn",
    )(x2, gain1, ada_l, ada_l, ada_l, gain2, w_in, w_out)


def _moe_kernel(nch_ref, cnt_ref, off_ref, h_ref, srow_ref, scol_ref, comb_ref, wi_ref, wo_ref, o_ref, xg):
    s = pl.program_id(0)
    e = pl.program_id(1)
    n_sub = MOE_SUPER // MOE_SUB
    group = s * N_EXPERTS + e
    n_units = nch_ref[group]

    @pl.when(e == 0)
    def _():
        o_ref[...] = jnp.zeros(o_ref.shape, F32)

    @pl.when(n_units > 0)
    def _():
        last = pl.multiple_of((n_units - 1) * MOE_TAIL, MOE_TAIL)
        xg[pl.ds(last, MOE_TAIL), :] = jnp.zeros((MOE_TAIL, D_MODEL), BF16)

    dest_col = lax.broadcasted_iota(jnp.int32, (MOE_PART, MOE_SUB), 0).astype(F32)
    dest_row = lax.broadcasted_iota(jnp.int32, (MOE_SUB, MOE_PART), 1).astype(F32)
    lane = lax.broadcasted_iota(jnp.int32, (MOE_SUB, LANES), 1)

    def gathered(j, part):
        srow = srow_ref[:, j * MOE_SUB:(j + 1) * MOE_SUB]
        onehot = jnp.where(srow - float(part * MOE_PART) == dest_col, 1.0, 0.0).astype(BF16)
        rows = jnp.dot(onehot, h_ref[j * MOE_SUB:(j + 1) * MOE_SUB, :], preferred_element_type=F32)
        return rows.astype(BF16)

    for j in range(n_sub):
        start = pl.multiple_of(off_ref[group * n_sub + j], MOE_ALIGN)
        xg[pl.ds(start, MOE_PART), :] = gathered(j, 0)

    part_row = lax.broadcasted_iota(jnp.int32, (MOE_PART, D_MODEL), 0)
    for j in range(n_sub):
        cnt = cnt_ref[group * n_sub + j]
        for part in range(1, MOE_SUB // MOE_PART):
            @pl.when(cnt > part * MOE_PART)
            def _():
                start = pl.multiple_of(off_ref[group * n_sub + j] + part * MOE_PART, MOE_ALIGN)
                cnt_pad = jnp.bitwise_and(cnt + (MOE_ALIGN - 1), -MOE_ALIGN)
                own = part_row < cnt_pad - part * MOE_PART
                xg[pl.ds(start, MOE_PART), :] = jnp.where(own, gathered(j, part), xg[pl.ds(start, MOE_PART), :])

    def swiglu_rows(r0, n_rows):
        x_e = xg[pl.ds(r0, n_rows), :]
        gate = jnp.dot(x_e, wi_ref[:, :D_FF_EXPERT], preferred_element_type=F32)
        up = jnp.dot(x_e, wi_ref[:, D_FF_EXPERT:], preferred_element_type=F32)
        a = (_silu(gate) * up).astype(BF16)
        xg[pl.ds(r0, n_rows), :] = jnp.dot(a, wo_ref[...], preferred_element_type=F32).astype(BF16)

    def chunk(c, carry):
        swiglu_rows(pl.multiple_of(c * MOE_CHUNK, MOE_CHUNK), MOE_CHUNK)
        return carry

    units_per_chunk = MOE_CHUNK // MOE_TAIL
    n_full = n_units // units_per_chunk
    lax.fori_loop(0, n_full, chunk, 0)
    done = n_full * units_per_chunk
    piece = units_per_chunk // 2
    while piece >= 1:
        take = (n_units - done) >= piece

        @pl.when(take)
        def _(done=done, piece=piece):
            swiglu_rows(pl.multiple_of(done * MOE_TAIL, MOE_TAIL), piece * MOE_TAIL)

        done = done + jnp.where(take, piece, 0)
        piece //= 2

    def scatter(j, part):
        rows = slice(j * MOE_SUB, (j + 1) * MOE_SUB)
        sel = lane == e
        scol = jnp.sum(jnp.where(sel, scol_ref[rows, :], 0.0), axis=-1, keepdims=True)
        prob = jnp.sum(jnp.where(sel, comb_ref[rows, :], 0.0), axis=-1, keepdims=True)
        start = pl.multiple_of(off_ref[group * n_sub + j] + part * MOE_PART, MOE_ALIGN)
        onehot = jnp.where(scol - float(part * MOE_PART) == dest_row, 1.0, 0.0).astype(BF16)
        y = xg[pl.ds(start, MOE_PART), :]
        if part > 0:
            cnt_pad = jnp.bitwise_and(cnt_ref[group * n_sub + j] + (MOE_ALIGN - 1), -MOE_ALIGN)
            y = jnp.where(part_row < cnt_pad - part * MOE_PART, y, jnp.zeros_like(y))
        o_ref[rows, :] += prob * jnp.dot(onehot, y, preferred_element_type=F32)

    for j in range(n_sub):
        scatter(j, 0)
    for j in range(n_sub):
        cnt = cnt_ref[group * n_sub + j]
        for part in range(1, MOE_SUB // MOE_PART):
            pl.when(cnt > part * MOE_PART)(functools.partial(scatter, j, part))


def _moe(h, comb, rank, w_in, w_out, layer):
    t = h.shape[0]
    n_super = t // MOE_SUPER
    n_sub = MOE_SUPER // MOE_SUB
    sel = (comb[:, :N_EXPERTS] > 0.0).reshape(n_super, n_sub, MOE_SUB, N_EXPERTS)
    cnt = sel.astype(jnp.int32).sum(axis=2)
    cnt_pad = (cnt + MOE_ALIGN - 1) // MOE_ALIGN * MOE_ALIGN
    off = jnp.cumsum(cnt_pad, axis=1) - cnt_pad
    total = cnt_pad.sum(axis=1)
    rank_rows = (rank[:, :N_EXPERTS].reshape(n_super, MOE_SUPER, N_EXPERTS).transpose(0, 2, 1)
                 .reshape(n_super, N_EXPERTS, 1, MOE_SUPER))
    n_units = ((total + MOE_TAIL - 1) // MOE_TAIL).reshape(-1).astype(jnp.int32)
    by_group = lambda a: a.transpose(0, 2, 1).reshape(-1).astype(jnp.int32)

    xg_rows = -(-(MOE_SUPER + n_sub * (MOE_ALIGN - 1) + MOE_PART) // MOE_CHUNK) * MOE_CHUNK
    grid_spec = pltpu.PrefetchScalarGridSpec(
        num_scalar_prefetch=3,
        grid=(n_super, N_EXPERTS),
        in_specs=[
            pl.BlockSpec((MOE_SUPER, D_MODEL), lambda s, e, *_: (s, 0)),
            pl.BlockSpec((None, None, 1, MOE_SUPER), lambda s, e, *_: (s, e, 0, 0)),
            pl.BlockSpec((MOE_SUPER, LANES), lambda s, e, *_: (s, 0)),
            pl.BlockSpec((MOE_SUPER, LANES), lambda s, e, *_: (s, 0)),
            pl.BlockSpec((None, D_MODEL, 2 * D_FF_EXPERT), lambda s, e, *_: (layer * N_EXPERTS + e, 0, 0)),
            pl.BlockSpec((None, D_FF_EXPERT, D_MODEL), lambda s, e, *_: (layer * N_EXPERTS + e, 0, 0)),
        ],
        out_specs=pl.BlockSpec((MOE_SUPER, D_MODEL), lambda s, e, *_: (s, 0)),
        scratch_shapes=[pltpu.VMEM((xg_rows, D_MODEL), BF16)],
    )
    return pl.pallas_call(
        _moe_kernel,
        grid_spec=grid_spec,
        out_shape=jax.ShapeDtypeStruct((t, D_MODEL), F32),
        compiler_params=_params(("arbitrary", "arbitrary"), 58),
        name="moe",
    )(n_units, by_group(cnt), by_group(off), h, rank_rows, rank, comb, w_in, w_out)


def _resid_kernel(x_ref, f_ref, gf_ref, g2_ref, o_ref):
    o_ref[...] = x_ref[...] + gf_ref[...] * _rms(f_ref[...], g2_ref[...])


def _resid(x2, f, ada_l, gain2, tiles_per_batch):
    t = x2.shape[0]
    tm = ROW_TILE
    row = lambda i: (i, 0)
    return pl.pallas_call(
        _resid_kernel,
        grid=(t // tm,),
        in_specs=[
            pl.BlockSpec((tm, D_MODEL), row),
            pl.BlockSpec((tm, D_MODEL), row),
            _ada_spec(5, tiles_per_batch),
            _resident((1, D_MODEL), lambda i: (0, 0)),
        ],
        out_specs=pl.BlockSpec((tm, D_MODEL), row),
        out_shape=jax.ShapeDtypeStruct((t, D_MODEL), F32),
        compiler_params=_params(("parallel",), 32),
        name="resid",
    )(x2, f, ada_l, gain2)


def kernel(x, c, w_ada, b_ada, pre_mix_gain, post_mix_gain, pre_ffn_gain, post_ffn_gain, w_in, conv_w, conv_b,
           w_rg_a, b_rg_a, w_rg_x, b_rg_x, lru_lambda, attn_sinks, w_branch, w_out, w_ffn_in, w_ffn_out,
           w_router, w_moe_in, w_moe_out):
    batch, seq, d = x.shape
    depth = w_in.shape[0]
    t = batch * seq
    tiles_per_batch = seq // ROW_TILE
    row1 = lambda v: v.reshape(1, -1)

    ada = _ada(c, w_ada, b_ada)
    n_routed = w_moe_in.shape[0]
    x2 = x.reshape(t, d)
    for l in range(depth):
        ada_l = ada[l]
        w_gate = jnp.concatenate([w_rg_a[l], w_rg_x[l]], axis=-1).astype(BF16)
        rnn_out, q, kv, gates = _inrnn(x2, row1(pre_mix_gain[l]), ada_l, w_in, l, conv_w[l], row1(conv_b[l]), w_gate,
                                       row1(b_rg_a[l]), row1(b_rg_x[l]), row1(lru_lambda[l]), tiles_per_batch)
        routed = min(l // 2, n_routed - 1)
        x2, expert_w = _attnmix(q, kv, attn_sinks[l], rnn_out, gates, x2, ada_l, row1(post_mix_gain[l]), w_branch,
                                w_out, l, tiles_per_batch, w_moe_in if l % 2 == 0 else w_moe_out, routed)
        if l % 2 == 0:
            w_moe_in_bf16 = expert_w
            x2 = _ffn(x2, ada_l, row1(pre_ffn_gain[l]), row1(post_ffn_gain[l]), w_ffn_in, w_ffn_out, l // 2,
                      tiles_per_batch)
        else:
            w_r = jnp.pad(w_router[l // 2], ((0, 0), (0, LANES - N_EXPERTS)))
            h, comb, rank = _router(x2, ada_l, row1(pre_ffn_gain[l]), w_r, tiles_per_batch)
            f = _moe(h, comb, rank, w_moe_in_bf16, expert_w, 0)
            x2 = _resid(x2, f, ada_l, row1(post_ffn_gain[l]), tiles_per_batch)
    return x2.reshape(batch, seq, d)
```

```python
import functools
import math

import jax
import jax.numpy as jnp
from jax import lax
from jax.experimental import pallas as pl
from jax.experimental.pallas import tpu as pltpu

D_MODEL = 1024
D_RNN = 1024
N_RNN_BLOCKS = 8
RNN_BLOCK = D_RNN // N_RNN_BLOCKS
CONV_WIDTH = 4
LRU_C = 8.0
N_HEADS = 8
N_KV_HEADS = 2
GROUP = N_HEADS // N_KV_HEADS
HEAD_DIM = 128
WINDOW = 128
D_ATTN = N_HEADS * HEAD_DIM
D_KV = N_KV_HEADS * HEAD_DIM
D_GATES = 2 * D_MODEL
D_IN = 2 * D_RNN + D_ATTN + 2 * D_KV + D_GATES
D_FF = 3 * D_MODEL
N_EXPERTS = 8
D_FF_EXPERT = D_FF // 2
EPS = 1e-6

BF16 = jnp.bfloat16
F32 = jnp.float32

SUBLANES = 8
LANES = 128
MIB = 1024 * 1024

ROW_TILE = 512
PROJ_CHUNK = 512
FF_CHUNK = 512
MOE_SUPER = 2048
MOE_SUB = 256
MOE_PART = 128
MOE_ALIGN = 16
MOE_CHUNK = 256
MOE_TAIL = 128
MASK_VALUE = -1e30
LOG2E = math.log2(math.e)


def _params(semantics, vmem_mib):
    return pltpu.CompilerParams(dimension_semantics=semantics, vmem_limit_bytes=vmem_mib * MIB)


def _resident(shape, index_map):
    return pl.BlockSpec(shape, index_map, pipeline_mode=pl.Buffered(1))


def _rms(x, gain):
    return x * lax.rsqrt(jnp.mean(x * x, axis=-1, keepdims=True) + EPS) * gain


def _silu(x):
    return x * jax.nn.sigmoid(x)


def _ada_kernel(c_ref, w_ref, b_ref, o_ref):
    w = w_ref[...]
    d, nb = w.shape
    for b in range(c_ref.shape[0]):
        col = _silu(c_ref[b])
        prod = w * jnp.concatenate([col] * (nb // LANES), axis=1)
        part = jnp.sum(prod.reshape(d // SUBLANES, SUBLANES, nb), axis=0)
        o_ref[b] = jnp.sum(part, axis=0, keepdims=True) + b_ref[...]


def _ada(c, w_ada, b_ada):
    n_layers, d, n6 = w_ada.shape
    b = c.shape[0]
    c_cols = jnp.broadcast_to(c[:, :, None], (b, d, LANES))
    nb = 1536
    return pl.pallas_call(
        _ada_kernel,
        grid=(n_layers, n6 // nb),
        in_specs=[
            pl.BlockSpec((b, d, LANES), lambda l, j: (0, 0, 0)),
            pl.BlockSpec((None, d, nb), lambda l, j: (l, 0, j)),
            pl.BlockSpec((None, 1, nb), lambda l, j: (l, 0, j)),
        ],
        out_specs=pl.BlockSpec((None, b, 1, nb), lambda l, j: (l, 0, 0, j)),
        out_shape=jax.ShapeDtypeStruct((n_layers, b, 1, n6), F32),
        compiler_params=_params(("arbitrary", "arbitrary"), 40),
        name="ada",
    )(c_cols, w_ada, b_ada.reshape(n_layers, 1, n6))


def _ada_spec(col, tiles_per_batch, lead=0):
    return pl.BlockSpec((None, 1, D_MODEL),
                        lambda i, *_: (jnp.maximum(i - lead, 0) // tiles_per_batch, 0, col))


W_STEPS = 8


def _row_spec(tm, width):
    return pl.BlockSpec((tm, width), lambda i: (jnp.maximum(i - W_STEPS, 0), 0))


def _weight_chunk_spec(layer, rows, cols):
    return pl.BlockSpec((None, rows // W_STEPS, cols), lambda i: (layer, jnp.minimum(i, W_STEPS - 1), 0))


def _keep_weight_chunk(i, w_ref, w_scr):
    rows = w_ref.shape[0]
    w_scr[pl.ds(pl.multiple_of(i * rows, rows), rows), :] = w_ref[...].astype(BF16)


def _segment_permutation(ts):
    seg = ts // SUBLANES
    p = jnp.arange(ts)
    src = (p % SUBLANES) * seg + p // SUBLANES
    perm = (src[:, None] == jnp.arange(ts)[None, :]).astype(BF16)
    return perm, perm.T


def _inrnn_kernel(*refs, tiles_per_batch, pending):
    (x_ref, gain_ref, sc_ref, sh_ref, w_ref, perm_ref, unperm_ref, cw_ref, cb_ref, wg_ref, ba_ref, bx_ref,
     lam_ref) = refs[:13]
    n_in = 16 if pending else 13
    o_ref, q_ref, kv_ref, gt_ref = refs[n_in:n_in + 4]
    w_scr, xs, gs, outp, hist, hcar = refs[-6:]
    step = pl.program_id(0)

    @pl.when(step < W_STEPS)
    def _():
        _keep_weight_chunk(step, w_ref, w_scr)

    @pl.when(step >= W_STEPS)
    def _():
        ts = x_ref.shape[0]
        seg = ts // SUBLANES

        @pl.when((step - W_STEPS) % tiles_per_batch == 0)
        def _():
            hist[...] = jnp.zeros((SUBLANES, D_RNN), F32)
            hcar[...] = jnp.zeros((SUBLANES, D_RNN), F32)

        x = x_ref[...]
        if pending:
            f_ref, gf_ref, g2_ref = refs[13:16]
            x = x + gf_ref[...] * _rms(f_ref[...], g2_ref[...])
            refs[n_in + 4][...] = x
        h_in = (_rms(x, gain_ref[...]) * (1.0 + sc_ref[...]) + sh_ref[...]).astype(BF16)
        hp = jnp.dot(perm_ref[...], h_in, preferred_element_type=F32).astype(BF16)
        xs[...] = jnp.dot(hp, w_scr[:, :D_RNN], preferred_element_type=F32)
        gs[...] = jnp.dot(hp, w_scr[:, D_RNN:2 * D_RNN], preferred_element_type=F32)
        proj_chunks = []
        col = 2 * D_RNN
        for ref, width in ((q_ref, D_ATTN), (kv_ref, 2 * D_KV), (gt_ref, D_GATES)):
            proj_chunks += [(ref, c, col + c) for c in range(0, width, PROJ_CHUNK)]
            col += width

        def project(ref, c, wcol):
            ref[:, c:c + PROJ_CHUNK] = jnp.dot(
                h_in, w_scr[:, wcol:wcol + PROJ_CHUNK], preferred_element_type=F32).astype(ref.dtype)

        sub = lax.broadcasted_iota(jnp.int32, (SUBLANES, RNN_BLOCK), 0)
        lam = lam_ref[...]
        sp2 = (-LRU_C * LOG2E) * (jnp.maximum(-lam, 0.0) + jnp.log1p(jnp.exp(-jnp.abs(lam))))
        c = math.sqrt(2.0 / math.pi)
        vrow = lambda arr, v: arr[v * SUBLANES:(v + 1) * SUBLANES, :]

        for n in range(N_RNN_BLOCKS):
            cols = slice(n * RNN_BLOCK, (n + 1) * RNN_BLOCK)
            xp = [xs[v * SUBLANES:(v + 1) * SUBLANES, cols] for v in range(seg)]

            def tail(j):
                rolled = pltpu.roll(xp[seg - j], 1, axis=0)
                return jnp.where(sub == 0, jnp.broadcast_to(hist[j:j + 1, cols], (SUBLANES, RNN_BLOCK)), rolled)

            tails = {j: tail(j) for j in range(1, CONV_WIDTH)}
            for j in range(1, CONV_WIDTH):
                hist[j:j + 1, cols] = xp[seg - j][SUBLANES - 1:SUBLANES, :]

            xc = cb_ref[:, cols] + cw_ref[CONV_WIDTH - 1:CONV_WIDTH, cols] * jnp.concatenate(xp, axis=0)
            for j in range(1, CONV_WIDTH):
                k = CONV_WIDTH - 1 - j
                shifted = jnp.concatenate([tails[j - v] for v in range(j)] + xp[:seg - j], axis=0)
                xc = xc + cw_ref[k:k + 1, cols] * shifted

            g = jnp.dot(xc.astype(BF16), wg_ref[n], preferred_element_type=F32)
            r = jax.nn.sigmoid(g[:, :RNN_BLOCK] + ba_ref[:, cols])
            i = jax.nn.sigmoid(g[:, RNN_BLOCK:] + bx_ref[:, cols])
            a = jnp.exp2(r * sp2[:, cols])
            w = 1.0 - a * a
            u = (w * lax.rsqrt(jnp.maximum(w, 1e-30))) * i * xc

            h = jnp.zeros((SUBLANES, RNN_BLOCK), F32)
            prod = jnp.ones((SUBLANES, RNN_BLOCK), F32)
            hs, prods = [], []
            for v in range(seg):
                h = vrow(a, v) * h + vrow(u, v)
                prod = vrow(a, v) * prod
                hs.append(h)
                prods.append(prod)

            pa, pb = prod, h
            for s in (1, 2, 4):
                a_sh = pltpu.roll(pa, s, axis=0)
                b_sh = pltpu.roll(pb, s, axis=0)
                m = sub >= s
                pb = jnp.where(m, pa * b_sh + pb, pb)
                pa = jnp.where(m, pa * a_sh, pa)
            h_end = pb + pa * hcar[:, cols]
            carry_in = jnp.where(sub == 0, hcar[:, cols], pltpu.roll(h_end, 1, axis=0))
            hcar[:, cols] = jnp.broadcast_to(h_end[SUBLANES - 1:SUBLANES, :], (SUBLANES, RNN_BLOCK))

            h_all = jnp.concatenate([hs[v] + prods[v] * carry_in for v in range(seg)], axis=0)

            gg = gs[:, cols]
            th = jnp.tanh(gg * (c + (c * 0.044715) * (gg * gg)))
            hg = h_all * (0.5 * gg)
            outp[:, cols] = (hg + hg * th).astype(BF16)
            if n < len(proj_chunks):
                project(*proj_chunks[n])

        for chunk in proj_chunks[N_RNN_BLOCKS:]:
            project(*chunk)
        o_ref[...] = jnp.dot(unperm_ref[...], outp[...], preferred_element_type=F32).astype(o_ref.dtype)


def _inrnn(x2, gain, ada_l, w_in, layer, conv_w, conv_b, w_gate, b_a, b_x, lam, tiles_per_batch, pending=None):
    t = x2.shape[0]
    tm = ROW_TILE
    widths = (D_RNN, D_ATTN, 2 * D_KV, D_GATES)
    perm, unperm = _segment_permutation(tm)
    const2 = lambda i: (0, 0)
    extra_specs, extra_args = [], []
    out_specs = [_row_spec(tm, w) for w in widths]
    out_shape = [jax.ShapeDtypeStruct((t, w), BF16) for w in widths]
    if pending is not None:
        f, ada_prev, gain2 = pending
        extra_specs = [_row_spec(tm, D_MODEL), _ada_spec(5, tiles_per_batch, W_STEPS), _resident((1, D_MODEL), const2)]
        extra_args = [f, ada_prev, gain2]
        out_specs.append(_row_spec(tm, D_MODEL))
        out_shape.append(jax.ShapeDtypeStruct((t, D_MODEL), F32))
    return pl.pallas_call(
        functools.partial(_inrnn_kernel, tiles_per_batch=tiles_per_batch, pending=pending is not None),
        grid=(W_STEPS + t // tm,),
        in_specs=[
            _row_spec(tm, D_MODEL),
            _resident((1, D_MODEL), const2),
            _ada_spec(1, tiles_per_batch, W_STEPS),
            _ada_spec(0, tiles_per_batch, W_STEPS),
            _weight_chunk_spec(layer, D_MODEL, D_IN),
            _resident((tm, tm), const2),
            _resident((tm, tm), const2),
            _resident((CONV_WIDTH, D_RNN), const2),
            _resident((1, D_RNN), const2),
            _resident((N_RNN_BLOCKS, RNN_BLOCK, 2 * RNN_BLOCK), lambda i: (0, 0, 0)),
            _resident((1, D_RNN), const2),
            _resident((1, D_RNN), const2),
            _resident((1, D_RNN), const2),
        ] + extra_specs,
        out_specs=out_specs,
        out_shape=out_shape,
        scratch_shapes=[
            pltpu.VMEM((D_MODEL, D_IN), BF16),
            pltpu.VMEM((tm, D_RNN), F32),
            pltpu.VMEM((tm, D_RNN), F32),
            pltpu.VMEM((tm, D_RNN), BF16),
            pltpu.VMEM((SUBLANES, D_RNN), F32),
            pltpu.VMEM((SUBLANES, D_RNN), F32),
        ],
        compiler_params=_params(("arbitrary",), 52),
        name="inrnn",
    )(x2, gain, ada_l, ada_l, w_in, perm, unperm, conv_w, conv_b, w_gate, b_a, b_x, lam, *extra_args)


def _attn_bias(sinks):
    qi = jnp.arange(WINDOW)[:, None]
    sj = jnp.arange(2 * WINDOW)[None, :]
    dist = (qi + WINDOW - sj).astype(F32)
    valid = (dist >= 0) & (dist < WINDOW)
    slopes = jnp.asarray([2.0 ** (-8.0 * (h + 1) / N_HEADS) for h in range(N_HEADS)], F32)
    bias = jnp.where(valid[None], -slopes[:, None, None] * dist[None], MASK_VALUE)
    bias = jnp.where((sj == 0)[None], sinks.astype(F32)[:, None, None], bias)
    return (bias * LOG2E).reshape(N_KV_HEADS, GROUP * WINDOW, 2 * WINDOW)


def _router_kernel(x_ref, g1_ref, sc_ref, sh_ref, wr_ref, h_ref, comb_ref, rank_ref):
    h = _rms(x_ref[...], g1_ref[...]) * (1.0 + sc_ref[...]) + sh_ref[...]
    h_hi = h.astype(BF16)
    h_ref[...] = h_hi
    h_lo = (h - h_hi.astype(F32)).astype(BF16)
    w = wr_ref[...]
    w_hi = w.astype(BF16)
    w_lo = (w - w_hi.astype(F32)).astype(BF16)
    dot = functools.partial(jnp.dot, preferred_element_type=F32)
    logits = dot(h_hi, w_hi) + (dot(h_lo, w_hi) + dot(h_hi, w_lo))
    lane = lax.broadcasted_iota(jnp.int32, logits.shape, 1).astype(F32)
    neg = jnp.float32(-jnp.inf)
    lg = jnp.where(lane < N_EXPERTS, logits, neg)
    m1 = jnp.max(lg, axis=-1, keepdims=True)
    i1 = jnp.min(jnp.where(lg == m1, lane, float(LANES)), axis=-1, keepdims=True)
    lg2 = jnp.where(lane == i1, neg, lg)
    m2 = jnp.max(lg2, axis=-1, keepdims=True)
    i2 = jnp.min(jnp.where(lg2 == m2, lane, float(LANES)), axis=-1, keepdims=True)
    e2 = jnp.exp(m2 - m1)
    p1 = 1.0 / (1.0 + e2)
    p2 = e2 / (1.0 + e2)
    comb = jnp.where(lane == i1, p1, 0.0) + jnp.where(lane == i2, p2, 0.0)
    comb_ref[...] = comb
    sel = jnp.where(comb > 0.0, 1.0, 0.0).astype(BF16)
    r = lax.broadcasted_iota(jnp.int32, (MOE_SUB, MOE_SUB), 0)
    c = lax.broadcasted_iota(jnp.int32, (MOE_SUB, MOE_SUB), 1)
    tri = jnp.where(c <= r, 1.0, 0.0).astype(BF16)
    for r0 in range(0, comb.shape[0], MOE_SUB):
        upto = jnp.dot(tri, sel[r0:r0 + MOE_SUB, :], preferred_element_type=F32)
        rank_ref[r0:r0 + MOE_SUB, :] = jnp.where(comb[r0:r0 + MOE_SUB, :] > 0.0, upto - 1.0, -1.0)


def _router(x2, ada_l, gain1, w_router_pad, tiles_per_batch):
    t = x2.shape[0]
    tm = ROW_TILE
    row = lambda i: (i, 0)
    return pl.pallas_call(
        _router_kernel,
        grid=(t // tm,),
        in_specs=[
            pl.BlockSpec((tm, D_MODEL), row),
            _resident((1, D_MODEL), lambda i: (0, 0)),
            _ada_spec(4, tiles_per_batch),
            _ada_spec(3, tiles_per_batch),
            _resident((D_MODEL, LANES), lambda i: (0, 0)),
        ],
        out_specs=[pl.BlockSpec((tm, D_MODEL), row), pl.BlockSpec((tm, LANES), row),
                   pl.BlockSpec((tm, LANES), row)],
        out_shape=[jax.ShapeDtypeStruct((t, D_MODEL), BF16), jax.ShapeDtypeStruct((t, LANES), F32),
                   jax.ShapeDtypeStruct((t, LANES), F32)],
        compiler_params=_params(("parallel",), 32),
        name="router",
    )(x2, gain1, ada_l, ada_l, w_router_pad)


def _attnmix_kernel(q_ref, kv_ref, kvp_ref, bias_ref, rnn_ref, gt_ref, x_ref, gm_ref, gain_ref, wb_ref, wo_ref,
                    cast_ref, o_ref, cast_out_ref, wb_scr, wo_scr, s_scr, p_scr, att_scr, m_scr, *,
                    tiles_per_batch):
    step = pl.program_id(0)

    @pl.when(step < W_STEPS)
    def _():
        _keep_weight_chunk(step, wb_ref, wb_scr)
        _keep_weight_chunk(step, wo_ref, wo_scr)

    @pl.when(step >= W_STEPS)
    def _():
        tq = q_ref.shape[0]
        first = (step - W_STEPS) % tiles_per_batch == 0
        col = lax.broadcasted_iota(jnp.int32, (GROUP * WINDOW, 2 * WINDOW), 1)
        first_mask = jnp.where(first & (col >= 1) & (col < WINDOW), MASK_VALUE, 0.0).astype(F32)
        slot0 = lax.broadcasted_iota(jnp.int32, (2 * WINDOW, HEAD_DIM), 0) == 0
        zeros = jnp.zeros((2 * WINDOW, HEAD_DIM), BF16)
        ones = jnp.ones((2 * WINDOW, HEAD_DIM), BF16)
        scale2 = (HEAD_DIM ** -0.5) * LOG2E
        units = [(jb, kvh) for jb in range(tq // WINDOW) for kvh in range(N_KV_HEADS)]

        def keys_or_values(jb, cols):
            r0 = jb * WINDOW
            prev = kvp_ref[:, cols] if jb == 0 else kv_ref[r0 - WINDOW:r0, cols]
            kv = jnp.concatenate([prev, kv_ref[r0:r0 + WINDOW, cols]], axis=0)
            return jnp.where(slot0, zeros, kv)

        for u, (jb, kvh) in enumerate(units):
            r0 = jb * WINDOW
            k = keys_or_values(jb, slice(kvh * HEAD_DIM, (kvh + 1) * HEAD_DIM))
            q = jnp.concatenate(
                [q_ref[r0:r0 + WINDOW, (kvh * GROUP + g) * HEAD_DIM:(kvh * GROUP + g + 1) * HEAD_DIM]
                 for g in range(GROUP)], axis=0)
            s = lax.dot_general(q, k, (((1,), (1,)), ((), ())), preferred_element_type=F32)
            s = s * scale2 + bias_ref[kvh]
            if jb == 0:
                s = s + first_mask
            s_scr[u] = s

        n_chunks = len(units) // 2
        width = D_MODEL // n_chunks

        def recurrent_branch(c):
            cols = slice(c * width, (c + 1) * width)
            bp0 = jnp.dot(rnn_ref[...], wb_scr[:D_MODEL, cols], preferred_element_type=F32)
            m_scr[:, cols] = jax.nn.sigmoid(gt_ref[:, cols].astype(F32)) * bp0

        for u, (jb, kvh) in enumerate(units):
            r0 = jb * WINDOW
            v = keys_or_values(jb, slice(D_KV + kvh * HEAD_DIM, D_KV + (kvh + 1) * HEAD_DIM))
            v_ext = jnp.concatenate([v, ones], axis=1)
            for g in range(GROUP):
                rows = slice(g * WINDOW, (g + 1) * WINDOW)
                s = s_scr[u, rows, :]
                m = jnp.max(s, axis=-1, keepdims=True)
                p_scr[u, rows, :] = jnp.exp2(s - m).astype(BF16)
            o_ext = jnp.dot(p_scr[u], v_ext, preferred_element_type=F32)
            o = o_ext[:, :HEAD_DIM] / o_ext[:, HEAD_DIM:]
            for g in range(GROUP):
                hd = kvh * GROUP + g
                att_scr[r0:r0 + WINDOW, hd * HEAD_DIM:(hd + 1) * HEAD_DIM] = (
                    o[g * WINDOW:(g + 1) * WINDOW, :].astype(BF16))
            if u % 2 == 1:
                recurrent_branch(u // 2)

        bp1 = jnp.dot(att_scr[...], wb_scr[D_MODEL:, :], preferred_element_type=F32)
        g1 = jax.nn.sigmoid(gt_ref[:, D_MODEL:].astype(F32))
        merged = (m_scr[...] + g1 * bp1).astype(BF16)
        mix = jnp.dot(merged, wo_scr[...], preferred_element_type=F32)
        o_ref[...] = x_ref[...] + gm_ref[...] * _rms(mix, gain_ref[...])
        cast_out_ref[...] = cast_ref[...].astype(BF16)


def _attnmix(q, kv, sinks, rnn_out, gates, x2, ada_l, gain, w_branch, w_out, layer, tiles_per_batch,
             expert_w, expert_layer):
    t = x2.shape[0]
    tm = ROW_TILE
    n_tiles = t // tm
    blocks_per_tile = tm // WINDOW
    n_units = blocks_per_tile * N_KV_HEADS
    _, n_exp, w_rows, w_cols = expert_w.shape
    cast_rows = n_exp * w_rows // n_tiles
    cast_block = lambda i: (expert_layer * n_tiles + jnp.maximum(i - W_STEPS, 0), 0)

    def prev_block(i):
        r = jnp.maximum(i - W_STEPS, 0)
        return (r * blocks_per_tile - jnp.where(r % tiles_per_batch == 0, 0, 1), 0)

    x_new, w_bf16 = pl.pallas_call(
        functools.partial(_attnmix_kernel, tiles_per_batch=tiles_per_batch),
        grid=(W_STEPS + n_tiles,),
        in_specs=[
            _row_spec(tm, D_ATTN),
            _row_spec(tm, 2 * D_KV),
            pl.BlockSpec((WINDOW, 2 * D_KV), prev_block),
            _resident((N_KV_HEADS, GROUP * WINDOW, 2 * WINDOW), lambda i: (0, 0, 0)),
            _row_spec(tm, D_MODEL),
            _row_spec(tm, D_GATES),
            _row_spec(tm, D_MODEL),
            _ada_spec(2, tiles_per_batch, W_STEPS),
            _resident((1, D_MODEL), lambda i: (0, 0)),
            _weight_chunk_spec(layer, 2 * D_MODEL, D_MODEL),
            _weight_chunk_spec(layer, D_MODEL, D_MODEL),
            pl.BlockSpec((cast_rows, w_cols), cast_block),
        ],
        out_specs=[_row_spec(tm, D_MODEL), _row_spec(cast_rows, w_cols)],
        out_shape=[jax.ShapeDtypeStruct((t, D_MODEL), F32),
                   jax.ShapeDtypeStruct((n_exp * w_rows, w_cols), BF16)],
        scratch_shapes=[
            pltpu.VMEM((2 * D_MODEL, D_MODEL), BF16),
            pltpu.VMEM((D_MODEL, D_MODEL), BF16),
            pltpu.VMEM((n_units, GROUP * WINDOW, 2 * WINDOW), F32),
            pltpu.VMEM((n_units, GROUP * WINDOW, 2 * WINDOW), BF16),
            pltpu.VMEM((tm, D_ATTN), BF16),
            pltpu.VMEM((tm, D_MODEL), F32),
        ],
        compiler_params=_params(("arbitrary",), 56),
        name="attnmix",
    )(q, kv, kv, _attn_bias(sinks), rnn_out, gates, x2, ada_l, gain,
      w_branch.reshape(-1, 2 * D_MODEL, D_MODEL), w_out, expert_w.reshape(-1, w_cols))
    return x_new, w_bf16.reshape(n_exp, w_rows, w_cols)


def _ffn_kernel(x_ref, g1_ref, sc_ref, sh_ref, gf_ref, g2_ref, wi_ref, wo_ref, o_ref, wi_scr, wo_scr, acc):
    i = pl.program_id(0)

    @pl.when(i < W_STEPS)
    def _():
        _keep_weight_chunk(i, wi_ref, wi_scr)
        _keep_weight_chunk(i, wo_ref, wo_scr)

    @pl.when(i >= W_STEPS)
    def _():
        x = x_ref[...]
        h = (_rms(x, g1_ref[...]) * (1.0 + sc_ref[...]) + sh_ref[...]).astype(BF16)
        for c in range(0, D_FF, FF_CHUNK):
            gate = jnp.dot(h, wi_scr[:, c:c + FF_CHUNK], preferred_element_type=F32)
            up = jnp.dot(h, wi_scr[:, D_FF + c:D_FF + c + FF_CHUNK], preferred_element_type=F32)
            a = (_silu(gate) * up).astype(BF16)
            part = jnp.dot(a, wo_scr[c:c + FF_CHUNK, :], preferred_element_type=F32)
            if c == 0:
                acc[...] = part
            else:
                acc[...] += part
        o_ref[...] = x + gf_ref[...] * _rms(acc[...], g2_ref[...])


def _ffn(x2, ada_l, gain1, gain2, w_in, w_out, layer, tiles_per_batch):
    t = x2.shape[0]
    tm = ROW_TILE
    return pl.pallas_call(
        _ffn_kernel,
        grid=(W_STEPS + t // tm,),
        in_specs=[
            _row_spec(tm, D_MODEL),
            _resident((1, D_MODEL), lambda i: (0, 0)),
            _ada_spec(4, tiles_per_batch, W_STEPS),
            _ada_spec(3, tiles_per_batch, W_STEPS),
            _ada_spec(5, tiles_per_batch, W_STEPS),
            _resident((1, D_MODEL), lambda i: (0, 0)),
            _weight_chunk_spec(layer, D_MODEL, 2 * D_FF),
            _weight_chunk_spec(layer, D_FF, D_MODEL),
        ],
        out_specs=_row_spec(tm, D_MODEL),
        out_shape=jax.ShapeDtypeStruct((t, D_MODEL), F32),
        scratch_shapes=[pltpu.VMEM((D_MODEL, 2 * D_FF), BF16), pltpu.VMEM((D_FF, D_MODEL), BF16),
                        pltpu.VMEM((tm, D_MODEL), F32)],
        compiler_params=_params(("arbitrary",), 56),
        name="ffn",
    )(x2, gain1, ada_l, ada_l, ada_l, gain2, w_in, w_out)


def _moe_kernel(nch_ref, cnt_ref, off_ref, h_ref, srow_ref, scol_ref, comb_ref, wi_ref, wo_ref, o_ref, xg):
    s = pl.program_id(0)
    e = pl.program_id(1)
    n_sub = MOE_SUPER // MOE_SUB
    group = s * N_EXPERTS + e
    n_units = nch_ref[group]

    @pl.when(e == 0)
    def _():
        o_ref[...] = jnp.zeros(o_ref.shape, F32)

    @pl.when(n_units > 0)
    def _():
        last = pl.multiple_of((n_units - 1) * MOE_TAIL, MOE_TAIL)
        xg[pl.ds(last, MOE_TAIL), :] = jnp.zeros((MOE_TAIL, D_MODEL), BF16)

    dest_col = lax.broadcasted_iota(jnp.int32, (MOE_PART, MOE_SUB), 0).astype(F32)
    dest_row = lax.broadcasted_iota(jnp.int32, (MOE_SUB, MOE_PART), 1).astype(F32)
    lane = lax.broadcasted_iota(jnp.int32, (MOE_SUB, LANES), 1)

    def gathered(j, part):
        srow = srow_ref[:, j * MOE_SUB:(j + 1) * MOE_SUB]
        onehot = jnp.where(srow - float(part * MOE_PART) == dest_col, 1.0, 0.0).astype(BF16)
        rows = jnp.dot(onehot, h_ref[j * MOE_SUB:(j + 1) * MOE_SUB, :], preferred_element_type=F32)
        return rows.astype(BF16)

    for j in range(n_sub):
        start = pl.multiple_of(off_ref[group * n_sub + j], MOE_ALIGN)
        xg[pl.ds(start, MOE_PART), :] = gathered(j, 0)

    part_row = lax.broadcasted_iota(jnp.int32, (MOE_PART, D_MODEL), 0)
    for j in range(n_sub):
        cnt = cnt_ref[group * n_sub + j]
        for part in range(1, MOE_SUB // MOE_PART):
            @pl.when(cnt > part * MOE_PART)
            def _():
                start = pl.multiple_of(off_ref[group * n_sub + j] + part * MOE_PART, MOE_ALIGN)
                cnt_pad = jnp.bitwise_and(cnt + (MOE_ALIGN - 1), -MOE_ALIGN)
                own = part_row < cnt_pad - part * MOE_PART
                xg[pl.ds(start, MOE_PART), :] = jnp.where(own, gathered(j, part), xg[pl.ds(start, MOE_PART), :])

    def swiglu_rows(r0, n_rows):
        x_e = xg[pl.ds(r0, n_rows), :]
        gate = jnp.dot(x_e, wi_ref[:, :D_FF_EXPERT], preferred_element_type=F32)
        up = jnp.dot(x_e, wi_ref[:, D_FF_EXPERT:], preferred_element_type=F32)
        a = (_silu(gate) * up).astype(BF16)
        xg[pl.ds(r0, n_rows), :] = jnp.dot(a, wo_ref[...], preferred_element_type=F32).astype(BF16)

    def chunk(c, carry):
        swiglu_rows(pl.multiple_of(c * MOE_CHUNK, MOE_CHUNK), MOE_CHUNK)
        return carry

    units_per_chunk = MOE_CHUNK // MOE_TAIL
    n_full = n_units // units_per_chunk
    lax.fori_loop(0, n_full, chunk, 0)
    done = n_full * units_per_chunk
    piece = units_per_chunk // 2
    while piece >= 1:
        take = (n_units - done) >= piece

        @pl.when(take)
        def _(done=done, piece=piece):
            swiglu_rows(pl.multiple_of(done * MOE_TAIL, MOE_TAIL), piece * MOE_TAIL)

        done = done + jnp.where(take, piece, 0)
        piece //= 2

    def scatter(j, part):
        rows = slice(j * MOE_SUB, (j + 1) * MOE_SUB)
        sel = lane == e
        scol = jnp.sum(jnp.where(sel, scol_ref[rows, :], 0.0), axis=-1, keepdims=True)
        prob = jnp.sum(jnp.where(sel, comb_ref[rows, :], 0.0), axis=-1, keepdims=True)
        start = pl.multiple_of(off_ref[group * n_sub + j] + part * MOE_PART, MOE_ALIGN)
        onehot = jnp.where(scol - float(part * MOE_PART) == dest_row, 1.0, 0.0).astype(BF16)
        y = xg[pl.ds(start, MOE_PART), :]
        if part > 0:
            cnt_pad = jnp.bitwise_and(cnt_ref[group * n_sub + j] + (MOE_ALIGN - 1), -MOE_ALIGN)
            y = jnp.where(part_row < cnt_pad - part * MOE_PART, y, jnp.zeros_like(y))
        o_ref[rows, :] += prob * jnp.dot(onehot, y, preferred_element_type=F32)

    for j in range(n_sub):
        scatter(j, 0)
    for j in range(n_sub):
        cnt = cnt_ref[group * n_sub + j]
        for part in range(1, MOE_SUB // MOE_PART):
            pl.when(cnt > part * MOE_PART)(functools.partial(scatter, j, part))


def _moe(h, comb, rank, w_in, w_out, layer):
    t = h.shape[0]
    n_super = t // MOE_SUPER
    n_sub = MOE_SUPER // MOE_SUB
    sel = (comb[:, :N_EXPERTS] > 0.0).reshape(n_super, n_sub, MOE_SUB, N_EXPERTS)
    cnt = sel.astype(jnp.int32).sum(axis=2)
    cnt_pad = (cnt + MOE_ALIGN - 1) // MOE_ALIGN * MOE_ALIGN
    off = jnp.cumsum(cnt_pad, axis=1) - cnt_pad
    total = cnt_pad.sum(axis=1)
    rank_rows = (rank[:, :N_EXPERTS].reshape(n_super, MOE_SUPER, N_EXPERTS).transpose(0, 2, 1)
                 .reshape(n_super, N_EXPERTS, 1, MOE_SUPER))
    n_units = ((total + MOE_TAIL - 1) // MOE_TAIL).reshape(-1).astype(jnp.int32)
    by_group = lambda a: a.transpose(0, 2, 1).reshape(-1).astype(jnp.int32)

    xg_rows = -(-(MOE_SUPER + n_sub * (MOE_ALIGN - 1) + MOE_PART) // MOE_CHUNK) * MOE_CHUNK
    grid_spec = pltpu.PrefetchScalarGridSpec(
        num_scalar_prefetch=3,
        grid=(n_super, N_EXPERTS),
        in_specs=[
            pl.BlockSpec((MOE_SUPER, D_MODEL), lambda s, e, *_: (s, 0)),
            pl.BlockSpec((None, None, 1, MOE_SUPER), lambda s, e, *_: (s, e, 0, 0)),
            pl.BlockSpec((MOE_SUPER, LANES), lambda s, e, *_: (s, 0)),
            pl.BlockSpec((MOE_SUPER, LANES), lambda s, e, *_: (s, 0)),
            pl.BlockSpec((None, D_MODEL, 2 * D_FF_EXPERT), lambda s, e, *_: (layer * N_EXPERTS + e, 0, 0)),
            pl.BlockSpec((None, D_FF_EXPERT, D_MODEL), lambda s, e, *_: (layer * N_EXPERTS + e, 0, 0)),
        ],
        out_specs=pl.BlockSpec((MOE_SUPER, D_MODEL), lambda s, e, *_: (s, 0)),
        scratch_shapes=[pltpu.VMEM((xg_rows, D_MODEL), BF16)],
    )
    return pl.pallas_call(
        _moe_kernel,
        grid_spec=grid_spec,
        out_shape=jax.ShapeDtypeStruct((t, D_MODEL), F32),
        compiler_params=_params(("arbitrary", "arbitrary"), 58),
        name="moe",
    )(n_units, by_group(cnt), by_group(off), h, rank_rows, rank, comb, w_in, w_out)


def _resid_kernel(x_ref, f_ref, gf_ref, g2_ref, o_ref):
    o_ref[...] = x_ref[...] + gf_ref[...] * _rms(f_ref[...], g2_ref[...])


def _resid(x2, f, ada_l, gain2, tiles_per_batch):
    t = x2.shape[0]
    tm = ROW_TILE
    row = lambda i: (i, 0)
    return pl.pallas_call(
        _resid_kernel,
        grid=(t // tm,),
        in_specs=[
            pl.BlockSpec((tm, D_MODEL), row),
            pl.BlockSpec((tm, D_MODEL), row),
            _ada_spec(5, tiles_per_batch),
            _resident((1, D_MODEL), lambda i: (0, 0)),
        ],
        out_specs=pl.BlockSpec((tm, D_MODEL), row),
        out_shape=jax.ShapeDtypeStruct((t, D_MODEL), F32),
        compiler_params=_params(("parallel",), 32),
        name="resid",
    )(x2, f, ada_l, gain2)


def kernel(x, c, w_ada, b_ada, pre_mix_gain, post_mix_gain, pre_ffn_gain, post_ffn_gain, w_in, conv_w, conv_b,
           w_rg_a, b_rg_a, w_rg_x, b_rg_x, lru_lambda, attn_sinks, w_branch, w_out, w_ffn_in, w_ffn_out,
           w_router, w_moe_in, w_moe_out):
    batch, seq, d = x.shape
    depth = w_in.shape[0]
    t = batch * seq
    tiles_per_batch = seq // ROW_TILE
    row1 = lambda v: v.reshape(1, -1)

    ada = _ada(c, w_ada, b_ada)
    n_routed = w_moe_in.shape[0]
    x2 = x.reshape(t, d)
    pending = None
    for l in range(depth):
        ada_l = ada[l]
        w_gate = jnp.concatenate([w_rg_a[l], w_rg_x[l]], axis=-1).astype(BF16)
        outs = _inrnn(x2, row1(pre_mix_gain[l]), ada_l, w_in, l, conv_w[l], row1(conv_b[l]), w_gate,
                      row1(b_rg_a[l]), row1(b_rg_x[l]), row1(lru_lambda[l]), tiles_per_batch, pending)
        rnn_out, q, kv, gates = outs[:4]
        if pending is not None:
            x2, pending = outs[4], None
        routed = min(l // 2, n_routed - 1)
        x2, expert_w = _attnmix(q, kv, attn_sinks[l], rnn_out, gates, x2, ada_l, row1(post_mix_gain[l]), w_branch,
                                w_out, l, tiles_per_batch, w_moe_in if l % 2 == 0 else w_moe_out, routed)
        if l % 2 == 0:
            w_moe_in_bf16 = expert_w
            x2 = _ffn(x2, ada_l, row1(pre_ffn_gain[l]), row1(post_ffn_gain[l]), w_ffn_in, w_ffn_out, l // 2,
                      tiles_per_batch)
        else:
            w_r = jnp.pad(w_router[l // 2], ((0, 0), (0, LANES - N_EXPERTS)))
            h, comb, rank = _router(x2, ada_l, row1(pre_ffn_gain[l]), w_r, tiles_per_batch)
            f = _moe(h, comb, rank, w_moe_in_bf16, expert_w, 0)
            if l + 1 < depth:
                pending = (f, ada_l, row1(post_ffn_gain[l]))
            else:
                x2 = _resid(x2, f, ada_l, row1(post_ffn_gain[l]), tiles_per_batch)
    return x2.reshape(batch, seq, d)
```

```python
import functools
import math

import jax
import jax.numpy as jnp
from jax import lax
from jax.experimental import pallas as pl
from jax.experimental.pallas import tpu as pltpu

D_MODEL = 1024
D_RNN = 1024
N_RNN_BLOCKS = 8
RNN_BLOCK = D_RNN // N_RNN_BLOCKS
CONV_WIDTH = 4
LRU_C = 8.0
N_HEADS = 8
N_KV_HEADS = 2
GROUP = N_HEADS // N_KV_HEADS
HEAD_DIM = 128
WINDOW = 128
D_ATTN = N_HEADS * HEAD_DIM
D_KV = N_KV_HEADS * HEAD_DIM
D_GATES = 2 * D_MODEL
D_IN = 2 * D_RNN + D_ATTN + 2 * D_KV + D_GATES
D_FF = 3 * D_MODEL
N_EXPERTS = 8
D_FF_EXPERT = D_FF // 2
EPS = 1e-6

BF16 = jnp.bfloat16
F32 = jnp.float32

SUBLANES = 8
LANES = 128
MIB = 1024 * 1024

ROW_TILE = 512
PROJ_CHUNK = 512
FF_CHUNK = 512
MOE_SUPER = 2048
MOE_SUB = 256
MOE_PART = 128
MOE_ALIGN = 16
MOE_CHUNK = 256
MOE_TAIL = 128
MASK_VALUE = -1e30
LOG2E = math.log2(math.e)


def _params(semantics, vmem_mib):
    return pltpu.CompilerParams(dimension_semantics=semantics, vmem_limit_bytes=vmem_mib * MIB)


def _resident(shape, index_map):
    return pl.BlockSpec(shape, index_map, pipeline_mode=pl.Buffered(1))


def _rms(x, gain):
    return x * lax.rsqrt(jnp.mean(x * x, axis=-1, keepdims=True) + EPS) * gain


def _silu(x):
    return x * jax.nn.sigmoid(x)


def _ada_kernel(c_ref, w_ref, b_ref, o_ref):
    w = w_ref[...]
    d, nb = w.shape
    for b in range(c_ref.shape[0]):
        col = _silu(c_ref[b])
        prod = w * jnp.concatenate([col] * (nb // LANES), axis=1)
        part = jnp.sum(prod.reshape(d // SUBLANES, SUBLANES, nb), axis=0)
        o_ref[b] = jnp.sum(part, axis=0, keepdims=True) + b_ref[...]


def _ada(c, w_ada, b_ada):
    n_layers, d, n6 = w_ada.shape
    b = c.shape[0]
    c_cols = jnp.broadcast_to(c[:, :, None], (b, d, LANES))
    nb = 1536
    return pl.pallas_call(
        _ada_kernel,
        grid=(n_layers, n6 // nb),
        in_specs=[
            pl.BlockSpec((b, d, LANES), lambda l, j: (0, 0, 0)),
            pl.BlockSpec((None, d, nb), lambda l, j: (l, 0, j)),
            pl.BlockSpec((None, 1, nb), lambda l, j: (l, 0, j)),
        ],
        out_specs=pl.BlockSpec((None, b, 1, nb), lambda l, j: (l, 0, 0, j)),
        out_shape=jax.ShapeDtypeStruct((n_layers, b, 1, n6), F32),
        compiler_params=_params(("arbitrary", "arbitrary"), 40),
        name="ada",
    )(c_cols, w_ada, b_ada.reshape(n_layers, 1, n6))


def _ada_spec(col, tiles_per_batch, lead=0):
    return pl.BlockSpec((None, 1, D_MODEL),
                        lambda i, *_: (jnp.maximum(i - lead, 0) // tiles_per_batch, 0, col))


W_STEPS = 4


def _row_spec(tm, width):
    return pl.BlockSpec((tm, width), lambda i: (jnp.maximum(i - W_STEPS, 0), 0))


def _weight_chunk_spec(layer, rows, cols):
    return pl.BlockSpec((None, rows // W_STEPS, cols), lambda i: (layer, jnp.minimum(i, W_STEPS - 1), 0))


def _keep_weight_chunk(i, w_ref, w_scr):
    rows = w_ref.shape[0]
    w_scr[pl.ds(pl.multiple_of(i * rows, rows), rows), :] = w_ref[...].astype(BF16)


def _segment_permutation(ts):
    seg = ts // SUBLANES
    p = jnp.arange(ts)
    src = (p % SUBLANES) * seg + p // SUBLANES
    perm = (src[:, None] == jnp.arange(ts)[None, :]).astype(BF16)
    return perm, perm.T


def _inrnn_kernel(*refs, tiles_per_batch, pending):
    (x_ref, gain_ref, sc_ref, sh_ref, w_ref, perm_ref, unperm_ref, cw_ref, cb_ref, wg_ref, ba_ref, bx_ref,
     lam_ref) = refs[:13]
    n_in = 16 if pending else 13
    o_ref, q_ref, kv_ref, gt_ref = refs[n_in:n_in + 4]
    w_scr, xs, gs, outp, hist, hcar = refs[-6:]
    step = pl.program_id(0)

    @pl.when(step < W_STEPS)
    def _():
        _keep_weight_chunk(step, w_ref, w_scr)

    @pl.when(step >= W_STEPS)
    def _():
        ts = x_ref.shape[0]
        seg = ts // SUBLANES

        @pl.when((step - W_STEPS) % tiles_per_batch == 0)
        def _():
            hist[...] = jnp.zeros((SUBLANES, D_RNN), F32)
            hcar[...] = jnp.zeros((SUBLANES, D_RNN), F32)

        x = x_ref[...]
        if pending:
            f_ref, gf_ref, g2_ref = refs[13:16]
            x = x + gf_ref[...] * _rms(f_ref[...], g2_ref[...])
            refs[n_in + 4][...] = x
        h_in = (_rms(x, gain_ref[...]) * (1.0 + sc_ref[...]) + sh_ref[...]).astype(BF16)
        hp = jnp.dot(perm_ref[...], h_in, preferred_element_type=F32).astype(BF16)
        xs[...] = jnp.dot(hp, w_scr[:, :D_RNN], preferred_element_type=F32)
        gs[...] = jnp.dot(hp, w_scr[:, D_RNN:2 * D_RNN], preferred_element_type=F32)
        proj_chunks = []
        col = 2 * D_RNN
        for ref, width in ((q_ref, D_ATTN), (kv_ref, 2 * D_KV), (gt_ref, D_GATES)):
            proj_chunks += [(ref, c, col + c) for c in range(0, width, PROJ_CHUNK)]
            col += width

        def project(ref, c, wcol):
            ref[:, c:c + PROJ_CHUNK] = jnp.dot(
                h_in, w_scr[:, wcol:wcol + PROJ_CHUNK], preferred_element_type=F32).astype(ref.dtype)

        sub = lax.broadcasted_iota(jnp.int32, (SUBLANES, RNN_BLOCK), 0)
        lam = lam_ref[...]
        sp2 = (-LRU_C * LOG2E) * (jnp.maximum(-lam, 0.0) + jnp.log1p(jnp.exp(-jnp.abs(lam))))
        c = math.sqrt(2.0 / math.pi)
        vrow = lambda arr, v: arr[v * SUBLANES:(v + 1) * SUBLANES, :]

        for n in range(N_RNN_BLOCKS):
            cols = slice(n * RNN_BLOCK, (n + 1) * RNN_BLOCK)
            xp = [xs[v * SUBLANES:(v + 1) * SUBLANES, cols] for v in range(seg)]

            def tail(j):
                rolled = pltpu.roll(xp[seg - j], 1, axis=0)
                return jnp.where(sub == 0, jnp.broadcast_to(hist[j:j + 1, cols], (SUBLANES, RNN_BLOCK)), rolled)

            tails = {j: tail(j) for j in range(1, CONV_WIDTH)}
            for j in range(1, CONV_WIDTH):
                hist[j:j + 1, cols] = xp[seg - j][SUBLANES - 1:SUBLANES, :]

            xc = cb_ref[:, cols] + cw_ref[CONV_WIDTH - 1:CONV_WIDTH, cols] * jnp.concatenate(xp, axis=0)
            for j in range(1, CONV_WIDTH):
                k = CONV_WIDTH - 1 - j
                shifted = jnp.concatenate([tails[j - v] for v in range(j)] + xp[:seg - j], axis=0)
                xc = xc + cw_ref[k:k + 1, cols] * shifted

            g = jnp.dot(xc.astype(BF16), wg_ref[n], preferred_element_type=F32)
            r = jax.nn.sigmoid(g[:, :RNN_BLOCK] + ba_ref[:, cols])
            i = jax.nn.sigmoid(g[:, RNN_BLOCK:] + bx_ref[:, cols])
            a = jnp.exp2(r * sp2[:, cols])
            w = 1.0 - a * a
            u = (w * lax.rsqrt(jnp.maximum(w, 1e-30))) * i * xc

            h = jnp.zeros((SUBLANES, RNN_BLOCK), F32)
            prod = jnp.ones((SUBLANES, RNN_BLOCK), F32)
            hs, prods = [], []
            for v in range(seg):
                h = vrow(a, v) * h + vrow(u, v)
                prod = vrow(a, v) * prod
                hs.append(h)
                prods.append(prod)

            pa, pb = prod, h
            for s in (1, 2, 4):
                a_sh = pltpu.roll(pa, s, axis=0)
                b_sh = pltpu.roll(pb, s, axis=0)
                m = sub >= s
                pb = jnp.where(m, pa * b_sh + pb, pb)
                pa = jnp.where(m, pa * a_sh, pa)
            h_end = pb + pa * hcar[:, cols]
            carry_in = jnp.where(sub == 0, hcar[:, cols], pltpu.roll(h_end, 1, axis=0))
            hcar[:, cols] = jnp.broadcast_to(h_end[SUBLANES - 1:SUBLANES, :], (SUBLANES, RNN_BLOCK))

            h_all = jnp.concatenate([hs[v] + prods[v] * carry_in for v in range(seg)], axis=0)

            gg = gs[:, cols]
            th = jnp.tanh(gg * (c + (c * 0.044715) * (gg * gg)))
            hg = h_all * (0.5 * gg)
            outp[:, cols] = (hg + hg * th).astype(BF16)
            if n < len(proj_chunks):
                project(*proj_chunks[n])

        for chunk in proj_chunks[N_RNN_BLOCKS:]:
            project(*chunk)
        o_ref[...] = jnp.dot(unperm_ref[...], outp[...], preferred_element_type=F32).astype(o_ref.dtype)


def _inrnn(x2, gain, ada_l, w_in, layer, conv_w, conv_b, w_gate, b_a, b_x, lam, tiles_per_batch, pending=None):
    t = x2.shape[0]
    tm = ROW_TILE
    widths = (D_RNN, D_ATTN, 2 * D_KV, D_GATES)
    perm, unperm = _segment_permutation(tm)
    const2 = lambda i: (0, 0)
    extra_specs, extra_args = [], []
    out_specs = [_row_spec(tm, w) for w in widths]
    out_shape = [jax.ShapeDtypeStruct((t, w), BF16) for w in widths]
    if pending is not None:
        f, ada_prev, gain2 = pending
        extra_specs = [_row_spec(tm, D_MODEL), _ada_spec(5, tiles_per_batch, W_STEPS), _resident((1, D_MODEL), const2)]
        extra_args = [f, ada_prev, gain2]
        out_specs.append(_row_spec(tm, D_MODEL))
        out_shape.append(jax.ShapeDtypeStruct((t, D_MODEL), F32))
    return pl.pallas_call(
        functools.partial(_inrnn_kernel, tiles_per_batch=tiles_per_batch, pending=pending is not None),
        grid=(W_STEPS + t // tm,),
        in_specs=[
            _row_spec(tm, D_MODEL),
            _resident((1, D_MODEL), const2),
            _ada_spec(1, tiles_per_batch, W_STEPS),
            _ada_spec(0, tiles_per_batch, W_STEPS),
            _weight_chunk_spec(layer, D_MODEL, D_IN),
            _resident((tm, tm), const2),
            _resident((tm, tm), const2),
            _resident((CONV_WIDTH, D_RNN), const2),
            _resident((1, D_RNN), const2),
            _resident((N_RNN_BLOCKS, RNN_BLOCK, 2 * RNN_BLOCK), lambda i: (0, 0, 0)),
            _resident((1, D_RNN), const2),
            _resident((1, D_RNN), const2),
            _resident((1, D_RNN), const2),
        ] + extra_specs,
        out_specs=out_specs,
        out_shape=out_shape,
        scratch_shapes=[
            pltpu.VMEM((D_MODEL, D_IN), BF16),
            pltpu.VMEM((tm, D_RNN), F32),
            pltpu.VMEM((tm, D_RNN), F32),
            pltpu.VMEM((tm, D_RNN), BF16),
            pltpu.VMEM((SUBLANES, D_RNN), F32),
            pltpu.VMEM((SUBLANES, D_RNN), F32),
        ],
        compiler_params=_params(("arbitrary",), 52),
        name="inrnn",
    )(x2, gain, ada_l, ada_l, w_in, perm, unperm, conv_w, conv_b, w_gate, b_a, b_x, lam, *extra_args)


def _attn_bias(sinks):
    qi = jnp.arange(WINDOW)[:, None]
    sj = jnp.arange(2 * WINDOW)[None, :]
    dist = (qi + WINDOW - sj).astype(F32)
    valid = (dist >= 0) & (dist < WINDOW)
    slopes = jnp.asarray([2.0 ** (-8.0 * (h + 1) / N_HEADS) for h in range(N_HEADS)], F32)
    bias = jnp.where(valid[None], -slopes[:, None, None] * dist[None], MASK_VALUE)
    bias = jnp.where((sj == 0)[None], sinks.astype(F32)[:, None, None], bias)
    return (bias * LOG2E).reshape(N_KV_HEADS, GROUP * WINDOW, 2 * WINDOW)


def _router_kernel(x_ref, g1_ref, sc_ref, sh_ref, wr_ref, h_ref, comb_ref, rank_ref):
    h = _rms(x_ref[...], g1_ref[...]) * (1.0 + sc_ref[...]) + sh_ref[...]
    h_hi = h.astype(BF16)
    h_ref[...] = h_hi
    h_lo = (h - h_hi.astype(F32)).astype(BF16)
    w = wr_ref[...]
    w_hi = w.astype(BF16)
    w_lo = (w - w_hi.astype(F32)).astype(BF16)
    dot = functools.partial(jnp.dot, preferred_element_type=F32)
    logits = dot(h_hi, w_hi) + (dot(h_lo, w_hi) + dot(h_hi, w_lo))
    lane = lax.broadcasted_iota(jnp.int32, logits.shape, 1).astype(F32)
    neg = jnp.float32(-jnp.inf)
    lg = jnp.where(lane < N_EXPERTS, logits, neg)
    m1 = jnp.max(lg, axis=-1, keepdims=True)
    i1 = jnp.min(jnp.where(lg == m1, lane, float(LANES)), axis=-1, keepdims=True)
    lg2 = jnp.where(lane == i1, neg, lg)
    m2 = jnp.max(lg2, axis=-1, keepdims=True)
    i2 = jnp.min(jnp.where(lg2 == m2, lane, float(LANES)), axis=-1, keepdims=True)
    e2 = jnp.exp(m2 - m1)
    p1 = 1.0 / (1.0 + e2)
    p2 = e2 / (1.0 + e2)
    comb = jnp.where(lane == i1, p1, 0.0) + jnp.where(lane == i2, p2, 0.0)
    comb_ref[...] = comb
    sel = jnp.where(comb > 0.0, 1.0, 0.0).astype(BF16)
    r = lax.broadcasted_iota(jnp.int32, (MOE_SUB, MOE_SUB), 0)
    c = lax.broadcasted_iota(jnp.int32, (MOE_SUB, MOE_SUB), 1)
    tri = jnp.where(c <= r, 1.0, 0.0).astype(BF16)
    for r0 in range(0, comb.shape[0], MOE_SUB):
        upto = jnp.dot(tri, sel[r0:r0 + MOE_SUB, :], preferred_element_type=F32)
        rank_ref[r0:r0 + MOE_SUB, :] = jnp.where(comb[r0:r0 + MOE_SUB, :] > 0.0, upto - 1.0, -1.0)


def _router(x2, ada_l, gain1, w_router_pad, tiles_per_batch):
    t = x2.shape[0]
    tm = ROW_TILE
    row = lambda i: (i, 0)
    return pl.pallas_call(
        _router_kernel,
        grid=(t // tm,),
        in_specs=[
            pl.BlockSpec((tm, D_MODEL), row),
            _resident((1, D_MODEL), lambda i: (0, 0)),
            _ada_spec(4, tiles_per_batch),
            _ada_spec(3, tiles_per_batch),
            _resident((D_MODEL, LANES), lambda i: (0, 0)),
        ],
        out_specs=[pl.BlockSpec((tm, D_MODEL), row), pl.BlockSpec((tm, LANES), row),
                   pl.BlockSpec((tm, LANES), row)],
        out_shape=[jax.ShapeDtypeStruct((t, D_MODEL), BF16), jax.ShapeDtypeStruct((t, LANES), F32),
                   jax.ShapeDtypeStruct((t, LANES), F32)],
        compiler_params=_params(("parallel",), 32),
        name="router",
    )(x2, gain1, ada_l, ada_l, w_router_pad)


def _attnmix_kernel(q_ref, kv_ref, kvp_ref, bias_ref, rnn_ref, gt_ref, x_ref, gm_ref, gain_ref, wb_ref, wo_ref,
                    cast_ref, o_ref, cast_out_ref, wb_scr, wo_scr, s_scr, p_scr, att_scr, m_scr, *,
                    tiles_per_batch):
    step = pl.program_id(0)

    @pl.when(step < W_STEPS)
    def _():
        _keep_weight_chunk(step, wb_ref, wb_scr)
        _keep_weight_chunk(step, wo_ref, wo_scr)

    @pl.when(step >= W_STEPS)
    def _():
        tq = q_ref.shape[0]
        first = (step - W_STEPS) % tiles_per_batch == 0
        col = lax.broadcasted_iota(jnp.int32, (GROUP * WINDOW, 2 * WINDOW), 1)
        first_mask = jnp.where(first & (col >= 1) & (col < WINDOW), MASK_VALUE, 0.0).astype(F32)
        slot0 = lax.broadcasted_iota(jnp.int32, (2 * WINDOW, HEAD_DIM), 0) == 0
        zeros = jnp.zeros((2 * WINDOW, HEAD_DIM), BF16)
        ones = jnp.ones((2 * WINDOW, HEAD_DIM), BF16)
        scale2 = (HEAD_DIM ** -0.5) * LOG2E
        units = [(jb, kvh) for jb in range(tq // WINDOW) for kvh in range(N_KV_HEADS)]

        def keys_or_values(jb, cols):
            r0 = jb * WINDOW
            prev = kvp_ref[:, cols] if jb == 0 else kv_ref[r0 - WINDOW:r0, cols]
            kv = jnp.concatenate([prev, kv_ref[r0:r0 + WINDOW, cols]], axis=0)
            return jnp.where(slot0, zeros, kv)

        for u, (jb, kvh) in enumerate(units):
            r0 = jb * WINDOW
            k = keys_or_values(jb, slice(kvh * HEAD_DIM, (kvh + 1) * HEAD_DIM))
            q = jnp.concatenate(
                [q_ref[r0:r0 + WINDOW, (kvh * GROUP + g) * HEAD_DIM:(kvh * GROUP + g + 1) * HEAD_DIM]
                 for g in range(GROUP)], axis=0)
            s = lax.dot_general(q, k, (((1,), (1,)), ((), ())), preferred_element_type=F32)
            s = s * scale2 + bias_ref[kvh]
            if jb == 0:
                s = s + first_mask
            s_scr[u] = s

        n_chunks = len(units) // 2
        width = D_MODEL // n_chunks

        def recurrent_branch(c):
            cols = slice(c * width, (c + 1) * width)
            bp0 = jnp.dot(rnn_ref[...], wb_scr[:D_MODEL, cols], preferred_element_type=F32)
            m_scr[:, cols] = jax.nn.sigmoid(gt_ref[:, cols].astype(F32)) * bp0

        for u, (jb, kvh) in enumerate(units):
            r0 = jb * WINDOW
            v = keys_or_values(jb, slice(D_KV + kvh * HEAD_DIM, D_KV + (kvh + 1) * HEAD_DIM))
            v_ext = jnp.concatenate([v, ones], axis=1)
            for g in range(GROUP):
                rows = slice(g * WINDOW, (g + 1) * WINDOW)
                s = s_scr[u, rows, :]
                m = jnp.max(s, axis=-1, keepdims=True)
                p_scr[u, rows, :] = jnp.exp2(s - m).astype(BF16)
            o_ext = jnp.dot(p_scr[u], v_ext, preferred_element_type=F32)
            o = o_ext[:, :HEAD_DIM] / o_ext[:, HEAD_DIM:]
            for g in range(GROUP):
                hd = kvh * GROUP + g
                att_scr[r0:r0 + WINDOW, hd * HEAD_DIM:(hd + 1) * HEAD_DIM] = (
                    o[g * WINDOW:(g + 1) * WINDOW, :].astype(BF16))
            if u % 2 == 1:
                recurrent_branch(u // 2)

        bp1 = jnp.dot(att_scr[...], wb_scr[D_MODEL:, :], preferred_element_type=F32)
        g1 = jax.nn.sigmoid(gt_ref[:, D_MODEL:].astype(F32))
        merged = (m_scr[...] + g1 * bp1).astype(BF16)
        mix = jnp.dot(merged, wo_scr[...], preferred_element_type=F32)
        o_ref[...] = x_ref[...] + gm_ref[...] * _rms(mix, gain_ref[...])
        cast_out_ref[...] = cast_ref[...].astype(BF16)


def _attnmix(q, kv, sinks, rnn_out, gates, x2, ada_l, gain, w_branch, w_out, layer, tiles_per_batch,
             expert_w, expert_layer):
    t = x2.shape[0]
    tm = ROW_TILE
    n_tiles = t // tm
    blocks_per_tile = tm // WINDOW
    n_units = blocks_per_tile * N_KV_HEADS
    _, n_exp, w_rows, w_cols = expert_w.shape
    cast_rows = n_exp * w_rows // n_tiles
    cast_block = lambda i: (expert_layer * n_tiles + jnp.maximum(i - W_STEPS, 0), 0)

    def prev_block(i):
        r = jnp.maximum(i - W_STEPS, 0)
        return (r * blocks_per_tile - jnp.where(r % tiles_per_batch == 0, 0, 1), 0)

    x_new, w_bf16 = pl.pallas_call(
        functools.partial(_attnmix_kernel, tiles_per_batch=tiles_per_batch),
        grid=(W_STEPS + n_tiles,),
        in_specs=[
            _row_spec(tm, D_ATTN),
            _row_spec(tm, 2 * D_KV),
            pl.BlockSpec((WINDOW, 2 * D_KV), prev_block),
            _resident((N_KV_HEADS, GROUP * WINDOW, 2 * WINDOW), lambda i: (0, 0, 0)),
            _row_spec(tm, D_MODEL),
            _row_spec(tm, D_GATES),
            _row_spec(tm, D_MODEL),
            _ada_spec(2, tiles_per_batch, W_STEPS),
            _resident((1, D_MODEL), lambda i: (0, 0)),
            _weight_chunk_spec(layer, 2 * D_MODEL, D_MODEL),
            _weight_chunk_spec(layer, D_MODEL, D_MODEL),
            pl.BlockSpec((cast_rows, w_cols), cast_block),
        ],
        out_specs=[_row_spec(tm, D_MODEL), _row_spec(cast_rows, w_cols)],
        out_shape=[jax.ShapeDtypeStruct((t, D_MODEL), F32),
                   jax.ShapeDtypeStruct((n_exp * w_rows, w_cols), BF16)],
        scratch_shapes=[
            pltpu.VMEM((2 * D_MODEL, D_MODEL), BF16),
            pltpu.VMEM((D_MODEL, D_MODEL), BF16),
            pltpu.VMEM((n_units, GROUP * WINDOW, 2 * WINDOW), F32),
            pltpu.VMEM((n_units, GROUP * WINDOW, 2 * WINDOW), BF16),
            pltpu.VMEM((tm, D_ATTN), BF16),
            pltpu.VMEM((tm, D_MODEL), F32),
        ],
        compiler_params=_params(("arbitrary",), 56),
        name="attnmix",
    )(q, kv, kv, _attn_bias(sinks), rnn_out, gates, x2, ada_l, gain,
      w_branch.reshape(-1, 2 * D_MODEL, D_MODEL), w_out, expert_w.reshape(-1, w_cols))
    return x_new, w_bf16.reshape(n_exp, w_rows, w_cols)


def _ffn_kernel(x_ref, g1_ref, sc_ref, sh_ref, gf_ref, g2_ref, wi_ref, wo_ref, o_ref, wi_scr, wo_scr, acc):
    i = pl.program_id(0)

    @pl.when(i < W_STEPS)
    def _():
        _keep_weight_chunk(i, wi_ref, wi_scr)
        _keep_weight_chunk(i, wo_ref, wo_scr)

    @pl.when(i >= W_STEPS)
    def _():
        x = x_ref[...]
        h = (_rms(x, g1_ref[...]) * (1.0 + sc_ref[...]) + sh_ref[...]).astype(BF16)
        for c in range(0, D_FF, FF_CHUNK):
            gate = jnp.dot(h, wi_scr[:, c:c + FF_CHUNK], preferred_element_type=F32)
            up = jnp.dot(h, wi_scr[:, D_FF + c:D_FF + c + FF_CHUNK], preferred_element_type=F32)
            a = (_silu(gate) * up).astype(BF16)
            part = jnp.dot(a, wo_scr[c:c + FF_CHUNK, :], preferred_element_type=F32)
            if c == 0:
                acc[...] = part
            else:
                acc[...] += part
        o_ref[...] = x + gf_ref[...] * _rms(acc[...], g2_ref[...])


def _ffn(x2, ada_l, gain1, gain2, w_in, w_out, layer, tiles_per_batch):
    t = x2.shape[0]
    tm = ROW_TILE
    return pl.pallas_call(
        _ffn_kernel,
        grid=(W_STEPS + t // tm,),
        in_specs=[
            _row_spec(tm, D_MODEL),
            _resident((1, D_MODEL), lambda i: (0, 0)),
            _ada_spec(4, tiles_per_batch, W_STEPS),
            _ada_spec(3, tiles_per_batch, W_STEPS),
            _ada_spec(5, tiles_per_batch, W_STEPS),
            _resident((1, D_MODEL), lambda i: (0, 0)),
            _weight_chunk_spec(layer, D_MODEL, 2 * D_FF),
            _weight_chunk_spec(layer, D_FF, D_MODEL),
        ],
        out_specs=_row_spec(tm, D_MODEL),
        out_shape=jax.ShapeDtypeStruct((t, D_MODEL), F32),
        scratch_shapes=[pltpu.VMEM((D_MODEL, 2 * D_FF), BF16), pltpu.VMEM((D_FF, D_MODEL), BF16),
                        pltpu.VMEM((tm, D_MODEL), F32)],
        compiler_params=_params(("arbitrary",), 56),
        name="ffn",
    )(x2, gain1, ada_l, ada_l, ada_l, gain2, w_in, w_out)


def _moe_kernel(nch_ref, cnt_ref, off_ref, h_ref, srow_ref, scol_ref, comb_ref, wi_ref, wo_ref, o_ref, xg):
    s = pl.program_id(0)
    e = pl.program_id(1)
    n_sub = MOE_SUPER // MOE_SUB
    group = s * N_EXPERTS + e
    n_units = nch_ref[group]

    @pl.when(e == 0)
    def _():
        o_ref[...] = jnp.zeros(o_ref.shape, F32)

    @pl.when(n_units > 0)
    def _():
        last = pl.multiple_of((n_units - 1) * MOE_TAIL, MOE_TAIL)
        xg[pl.ds(last, MOE_TAIL), :] = jnp.zeros((MOE_TAIL, D_MODEL), BF16)

    dest_col = lax.broadcasted_iota(jnp.int32, (MOE_PART, MOE_SUB), 0).astype(F32)
    dest_row = lax.broadcasted_iota(jnp.int32, (MOE_SUB, MOE_PART), 1).astype(F32)
    lane = lax.broadcasted_iota(jnp.int32, (MOE_SUB, LANES), 1)

    def gathered(j, part):
        srow = srow_ref[:, j * MOE_SUB:(j + 1) * MOE_SUB]
        onehot = jnp.where(srow - float(part * MOE_PART) == dest_col, 1.0, 0.0).astype(BF16)
        rows = jnp.dot(onehot, h_ref[j * MOE_SUB:(j + 1) * MOE_SUB, :], preferred_element_type=F32)
        return rows.astype(BF16)

    for j in range(n_sub):
        start = pl.multiple_of(off_ref[group * n_sub + j], MOE_ALIGN)
        xg[pl.ds(start, MOE_PART), :] = gathered(j, 0)

    part_row = lax.broadcasted_iota(jnp.int32, (MOE_PART, D_MODEL), 0)
    for j in range(n_sub):
        cnt = cnt_ref[group * n_sub + j]
        for part in range(1, MOE_SUB // MOE_PART):
            @pl.when(cnt > part * MOE_PART)
            def _():
                start = pl.multiple_of(off_ref[group * n_sub + j] + part * MOE_PART, MOE_ALIGN)
                cnt_pad = jnp.bitwise_and(cnt + (MOE_ALIGN - 1), -MOE_ALIGN)
                own = part_row < cnt_pad - part * MOE_PART
                xg[pl.ds(start, MOE_PART), :] = jnp.where(own, gathered(j, part), xg[pl.ds(start, MOE_PART), :])

    def swiglu_rows(r0, n_rows):
        x_e = xg[pl.ds(r0, n_rows), :]
        gate = jnp.dot(x_e, wi_ref[:, :D_FF_EXPERT], preferred_element_type=F32)
        up = jnp.dot(x_e, wi_ref[:, D_FF_EXPERT:], preferred_element_type=F32)
        a = (_silu(gate) * up).astype(BF16)
        xg[pl.ds(r0, n_rows), :] = jnp.dot(a, wo_ref[...], preferred_element_type=F32).astype(BF16)

    def chunk(c, carry):
        swiglu_rows(pl.multiple_of(c * MOE_CHUNK, MOE_CHUNK), MOE_CHUNK)
        return carry

    units_per_chunk = MOE_CHUNK // MOE_TAIL
    n_full = n_units // units_per_chunk
    lax.fori_loop(0, n_full, chunk, 0)
    done = n_full * units_per_chunk
    piece = units_per_chunk // 2
    while piece >= 1:
        take = (n_units - done) >= piece

        @pl.when(take)
        def _(done=done, piece=piece):
            swiglu_rows(pl.multiple_of(done * MOE_TAIL, MOE_TAIL), piece * MOE_TAIL)

        done = done + jnp.where(take, piece, 0)
        piece //= 2

    def scatter(j, part):
        rows = slice(j * MOE_SUB, (j + 1) * MOE_SUB)
        sel = lane == e
        scol = jnp.sum(jnp.where(sel, scol_ref[rows, :], 0.0), axis=-1, keepdims=True)
        prob = jnp.sum(jnp.where(sel, comb_ref[rows, :], 0.0), axis=-1, keepdims=True)
        start = pl.multiple_of(off_ref[group * n_sub + j] + part * MOE_PART, MOE_ALIGN)
        onehot = jnp.where(scol - float(part * MOE_PART) == dest_row, 1.0, 0.0).astype(BF16)
        y = xg[pl.ds(start, MOE_PART), :]
        if part > 0:
            cnt_pad = jnp.bitwise_and(cnt_ref[group * n_sub + j] + (MOE_ALIGN - 1), -MOE_ALIGN)
            y = jnp.where(part_row < cnt_pad - part * MOE_PART, y, jnp.zeros_like(y))
        o_ref[rows, :] += prob * jnp.dot(onehot, y, preferred_element_type=F32)

    for j in range(n_sub):
        scatter(j, 0)
    for j in range(n_sub):
        cnt = cnt_ref[group * n_sub + j]
        for part in range(1, MOE_SUB // MOE_PART):
            pl.when(cnt > part * MOE_PART)(functools.partial(scatter, j, part))


def _moe(h, comb, rank, w_in, w_out, layer):
    t = h.shape[0]
    n_super = t // MOE_SUPER
    n_sub = MOE_SUPER // MOE_SUB
    sel = (comb[:, :N_EXPERTS] > 0.0).reshape(n_super, n_sub, MOE_SUB, N_EXPERTS)
    cnt = sel.astype(jnp.int32).sum(axis=2)
    cnt_pad = (cnt + MOE_ALIGN - 1) // MOE_ALIGN * MOE_ALIGN
    off = jnp.cumsum(cnt_pad, axis=1) - cnt_pad
    total = cnt_pad.sum(axis=1)
    rank_rows = (rank[:, :N_EXPERTS].reshape(n_super, MOE_SUPER, N_EXPERTS).transpose(0, 2, 1)
                 .reshape(n_super, N_EXPERTS, 1, MOE_SUPER))
    n_units = ((total + MOE_TAIL - 1) // MOE_TAIL).reshape(-1).astype(jnp.int32)
    by_group = lambda a: a.transpose(0, 2, 1).reshape(-1).astype(jnp.int32)

    xg_rows = -(-(MOE_SUPER + n_sub * (MOE_ALIGN - 1) + MOE_PART) // MOE_CHUNK) * MOE_CHUNK
    grid_spec = pltpu.PrefetchScalarGridSpec(
        num_scalar_prefetch=3,
        grid=(n_super, N_EXPERTS),
        in_specs=[
            pl.BlockSpec((MOE_SUPER, D_MODEL), lambda s, e, *_: (s, 0)),
            pl.BlockSpec((None, None, 1, MOE_SUPER), lambda s, e, *_: (s, e, 0, 0)),
            pl.BlockSpec((MOE_SUPER, LANES), lambda s, e, *_: (s, 0)),
            pl.BlockSpec((MOE_SUPER, LANES), lambda s, e, *_: (s, 0)),
            pl.BlockSpec((None, D_MODEL, 2 * D_FF_EXPERT), lambda s, e, *_: (layer * N_EXPERTS + e, 0, 0)),
            pl.BlockSpec((None, D_FF_EXPERT, D_MODEL), lambda s, e, *_: (layer * N_EXPERTS + e, 0, 0)),
        ],
        out_specs=pl.BlockSpec((MOE_SUPER, D_MODEL), lambda s, e, *_: (s, 0)),
        scratch_shapes=[pltpu.VMEM((xg_rows, D_MODEL), BF16)],
    )
    return pl.pallas_call(
        _moe_kernel,
        grid_spec=grid_spec,
        out_shape=jax.ShapeDtypeStruct((t, D_MODEL), F32),
        compiler_params=_params(("arbitrary", "arbitrary"), 58),
        name="moe",
    )(n_units, by_group(cnt), by_group(off), h, rank_rows, rank, comb, w_in, w_out)


def _resid_kernel(x_ref, f_ref, gf_ref, g2_ref, o_ref):
    o_ref[...] = x_ref[...] + gf_ref[...] * _rms(f_ref[...], g2_ref[...])


def _resid(x2, f, ada_l, gain2, tiles_per_batch):
    t = x2.shape[0]
    tm = ROW_TILE
    row = lambda i: (i, 0)
    return pl.pallas_call(
        _resid_kernel,
        grid=(t // tm,),
        in_specs=[
            pl.BlockSpec((tm, D_MODEL), row),
            pl.BlockSpec((tm, D_MODEL), row),
            _ada_spec(5, tiles_per_batch),
            _resident((1, D_MODEL), lambda i: (0, 0)),
        ],
        out_specs=pl.BlockSpec((tm, D_MODEL), row),
        out_shape=jax.ShapeDtypeStruct((t, D_MODEL), F32),
        compiler_params=_params(("parallel",), 32),
        name="resid",
    )(x2, f, ada_l, gain2)


def kernel(x, c, w_ada, b_ada, pre_mix_gain, post_mix_gain, pre_ffn_gain, post_ffn_gain, w_in, conv_w, conv_b,
           w_rg_a, b_rg_a, w_rg_x, b_rg_x, lru_lambda, attn_sinks, w_branch, w_out, w_ffn_in, w_ffn_out,
           w_router, w_moe_in, w_moe_out):
    batch, seq, d = x.shape
    depth = w_in.shape[0]
    t = batch * seq
    tiles_per_batch = seq // ROW_TILE
    row1 = lambda v: v.reshape(1, -1)

    ada = _ada(c, w_ada, b_ada)
    n_routed = w_moe_in.shape[0]
    x2 = x.reshape(t, d)
    pending = None
    for l in range(depth):
        ada_l = ada[l]
        w_gate = jnp.concatenate([w_rg_a[l], w_rg_x[l]], axis=-1).astype(BF16)
        outs = _inrnn(x2, row1(pre_mix_gain[l]), ada_l, w_in, l, conv_w[l], row1(conv_b[l]), w_gate,
                      row1(b_rg_a[l]), row1(b_rg_x[l]), row1(lru_lambda[l]), tiles_per_batch, pending)
        rnn_out, q, kv, gates = outs[:4]
        if pending is not None:
            x2, pending = outs[4], None
        routed = min(l // 2, n_routed - 1)
        x2, expert_w = _attnmix(q, kv, attn_sinks[l], rnn_out, gates, x2, ada_l, row1(post_mix_gain[l]), w_branch,
                                w_out, l, tiles_per_batch, w_moe_in if l % 2 == 0 else w_moe_out, routed)
        if l % 2 == 0:
            w_moe_in_bf16 = expert_w
            x2 = _ffn(x2, ada_l, row1(pre_ffn_gain[l]), row1(post_ffn_gain[l]), w_ffn_in, w_ffn_out, l // 2,
                      tiles_per_batch)
        else:
            w_r = jnp.pad(w_router[l // 2], ((0, 0), (0, LANES - N_EXPERTS)))
            h, comb, rank = _router(x2, ada_l, row1(pre_ffn_gain[l]), w_r, tiles_per_batch)
            f = _moe(h, comb, rank, w_moe_in_bf16, expert_w, 0)
            if l + 1 < depth:
                pending = (f, ada_l, row1(post_ffn_gain[l]))
            else:
                x2 = _resid(x2, f, ada_l, row1(post_ffn_gain[l]), tiles_per_batch)
    return x2.reshape(batch, seq, d)
```
